```python
import math
import jax, jax.numpy as jnp
from jax import lax
import numpy as np

D_MODEL = 1024
BATCH = 16
SEQ = 4096
DEPTH = 1

CHUNK = 64
Q_BLOCK = 128
FOX_HEADS = 8
FOX_HEAD_DIM = 64
FOX_WIDTH = FOX_HEADS * FOX_HEAD_DIM
RET_HEADS = 4
RET_HEAD_DIM = 128
RET_WIDTH = RET_HEADS * RET_HEAD_DIM
D_MIX = FOX_WIDTH + RET_WIDTH
SPLIT_SIZES = [FOX_WIDTH, FOX_WIDTH, FOX_WIDTH, FOX_WIDTH, FOX_HEADS, RET_WIDTH, RET_WIDTH, RET_WIDTH, RET_WIDTH]
D_IN = sum(SPLIT_SIZES)
ROPE_BASE = 10000.0
NORM_EPS = 1e-6
GN_EPS = 1e-5

kernel_name = "hybrid_fox_retention_block"


def rms_norm(x, g):
    xf = x.astype(jnp.float32)
    y = xf * lax.rsqrt(jnp.mean(xf * xf, axis=-1, keepdims=True) + NORM_EPS)
    return (y * g.astype(jnp.float32)).astype(x.dtype)


def forgetting_attention(q, k, v, f_logit):
    b, s, h, dh = q.shape
    scale = dh ** -0.5
    log_f = jax.nn.log_sigmoid(f_logit.astype(jnp.float32))
    c = jnp.cumsum(log_f, axis=1).transpose(0, 2, 1)
    qh = q.transpose(0, 2, 1, 3)
    kh = k.transpose(0, 2, 1, 3)
    vh = v.transpose(0, 2, 1, 3)
    outs = []
    for blk in range(s // Q_BLOCK):
        q0 = blk * Q_BLOCK
        q1 = q0 + Q_BLOCK
        qb = qh[:, :, q0:q1].astype(jnp.float32)
        kb = kh[:, :, :q1].astype(jnp.float32)
        vb = vh[:, :, :q1]
        logits = jnp.einsum('bhqd,bhkd->bhqk', qb, kb) * scale
        logits = logits + c[:, :, q0:q1, None] - c[:, :, None, :q1]
        qpos = jnp.arange(q0, q1)[:, None]
        kpos = jnp.arange(q1)[None, :]
        logits = jnp.where(kpos <= qpos, logits, -jnp.inf)
        p = jax.nn.softmax(logits, axis=-1)
        outs.append(jnp.einsum('bhqk,bhkd->bhqd', p.astype(vb.dtype), vb))
    o = jnp.concatenate(outs, axis=2)
    return o.transpose(0, 2, 1, 3)


def rotary(x, pos):
    d = x.shape[-1]
    half = d // 2
    inv = 1.0 / (ROPE_BASE ** (jnp.arange(half, dtype=jnp.float32) / half))
    ang = pos.astype(jnp.float32)[:, None] * inv[None, :]
    cos = jnp.cos(ang)[None, :, None, :]
    sin = jnp.sin(ang)[None, :, None, :]
    xf = x.astype(jnp.float32)
    x1, x2 = xf[..., :half], xf[..., half:]
    return jnp.concatenate([x1 * cos - x2 * sin, x1 * sin + x2 * cos], axis=-1)


def retention(q, k, v):
    b, s, h, dk = q.shape
    dv = v.shape[-1]
    nc = s // CHUNK
    log_gamma = jnp.log1p(-jnp.exp(jnp.linspace(math.log(1.0 / 32), math.log(1.0 / 512), h))).astype(jnp.float32)
    pos = jnp.arange(s)
    qf = rotary(q, pos)
    kf = rotary(k, pos) * (dk ** -0.5)
    vf = v.astype(jnp.float32)
    qc = qf.reshape(b, nc, CHUNK, h, dk)
    kc = kf.reshape(b, nc, CHUNK, h, dk)
    vc = vf.reshape(b, nc, CHUNK, h, dv)
    idx = jnp.arange(CHUNK, dtype=jnp.float32)
    dist = jnp.abs(idx[:, None] - idx[None, :])
    decay_in = jnp.exp(log_gamma[:, None, None] * dist)
    scores = jnp.einsum('bnihd,bnjhd->bnhij', qc, kc) * decay_in
    inner = jnp.einsum('bnhij,bnjhv->bnihv', scores, vc)
    q_decay = jnp.exp(log_gamma[None, :] * (idx[:, None] + 1.0))
    k_decay = jnp.exp(log_gamma[None, :] * (CHUNK - 1.0 - idx[:, None]))
    chunk_decay = jnp.exp(log_gamma * CHUNK)

    def step(state, xs):
        qi, ki, vi = xs
        cross = jnp.einsum('bihk,bhkv->bihv', qi, state) * q_decay[None, :, :, None]
        state = state * chunk_decay[None, :, None, None] + jnp.einsum('bjhk,bjhv->bhkv', ki * k_decay[None, :, :, None], vi)
        return state, cross

    state0 = jnp.zeros((b, h, dk, dv), jnp.float32)
    _, cross = lax.scan(step, state0, (qc.transpose(1, 0, 2, 3, 4), kc.transpose(1, 0, 2, 3, 4), vc.transpose(1, 0, 2, 3, 4)))
    cross = cross.transpose(1, 0, 2, 3, 4)
    o = (inner + cross).reshape(b, s, h, dv)
    mu = jnp.mean(o, axis=-1, keepdims=True)
    var = jnp.mean(jnp.square(o - mu), axis=-1, keepdims=True)
    return (o - mu) * lax.rsqrt(var + GN_EPS)


def setup_inputs(seed: int = 0) -> dict:
    key = jax.random.key(seed)
    kx, kg1, kin, kfb, kout, kg2 = jax.random.split(key, 6)
    x = jax.random.normal(kx, (BATCH, SEQ, D_MODEL), jnp.float32)
    pre_norm_gain = 1.0 + 0.02 * jax.random.normal(kg1, (DEPTH, D_MODEL), jnp.float32)
    w_in = jax.random.normal(kin, (DEPTH, D_MODEL, D_IN), jnp.float32) * (D_MODEL ** -0.5)
    fox_forget_bias = jnp.linspace(1.0, 6.0, FOX_HEADS, dtype=jnp.float32)[None, :] + 0.01 * jax.random.normal(kfb, (DEPTH, FOX_HEADS), jnp.float32)
    w_out = jax.random.normal(kout, (DEPTH, D_MIX, D_MODEL), jnp.float32) * (D_MIX ** -0.5)
    post_norm_gain = 1.0 + 0.02 * jax.random.normal(kg2, (DEPTH, D_MODEL), jnp.float32)
    return {"x": x, "pre_norm_gain": pre_norm_gain, "w_in": w_in, "fox_forget_bias": fox_forget_bias, "w_out": w_out, "post_norm_gain": post_norm_gain}


def reference(x, pre_norm_gain, w_in, fox_forget_bias, w_out, post_norm_gain):
    b, s, _ = x.shape
    split_points = [int(v) for v in np.cumsum(SPLIT_SIZES)[:-1]]
    h = x
    for layer in range(DEPTH):
        u = rms_norm(h, pre_norm_gain[layer])
        proj = jnp.einsum('bsd,de->bse', u, w_in[layer])
        fq, fk, fv, fz, ff, rq, rk, rv, rz = jnp.split(proj, split_points, axis=-1)
        f_logit = ff + fox_forget_bias[layer].astype(ff.dtype)
        fox = forgetting_attention(fq.reshape(b, s, FOX_HEADS, FOX_HEAD_DIM), fk.reshape(b, s, FOX_HEADS, FOX_HEAD_DIM), fv.reshape(b, s, FOX_HEADS, FOX_HEAD_DIM), f_logit)
        fox = fox.reshape(b, s, FOX_WIDTH).astype(x.dtype) * jax.nn.silu(fz)
        ret = retention(rq.reshape(b, s, RET_HEADS, RET_HEAD_DIM), rk.reshape(b, s, RET_HEADS, RET_HEAD_DIM), rv.reshape(b, s, RET_HEADS, RET_HEAD_DIM))
        ret = ret.reshape(b, s, RET_WIDTH).astype(x.dtype) * jax.nn.silu(rz)
        mixed = jnp.concatenate([fox, ret], axis=-1)
        out = jnp.einsum('bse,ed->bsd', mixed, w_out[layer])
        h = h + rms_norm(out, post_norm_gain[layer])
    return h
```

```python
import functools
import math

import jax
import jax.numpy as jnp
from jax import lax
from jax.experimental import pallas as pl
from jax.experimental.pallas import tpu as pltpu

D_MODEL = 1024
CHUNK = 64
FOX_HEADS = 8
FOX_HEAD_DIM = 64
FOX_WIDTH = FOX_HEADS * FOX_HEAD_DIM
RET_HEADS = 4
RET_HEAD_DIM = 128
RET_WIDTH = RET_HEADS * RET_HEAD_DIM
ROPE_BASE = 10000.0
NORM_EPS = 1e-6
GN_EPS = 1e-5

LANES = 128
LOG2E = math.log2(math.e)
NEG_BIG = -1e30

IN_ROWS = 512
FOX_BLOCK = 512
RET_BLOCK = 256
OUT_ROWS = 512
VMEM_LIMIT = 48 * 1024 * 1024


def _dot(a, b):
    return jnp.dot(a, b, preferred_element_type=jnp.float32)


def _dot_nt(a, b):
    return lax.dot_general(a, b, (((1,), (1,)), ((), ())), preferred_element_type=jnp.float32)


def _dot_tn(a, b):
    return lax.dot_general(a, b, (((0,), (0,)), ((), ())), preferred_element_type=jnp.float32)


def _rope_table_kernel(inv_ref, cos_ref, sin_ref):
    rows = cos_ref.shape[0]
    pos = (lax.broadcasted_iota(jnp.int32, (rows, LANES), 0) + pl.program_id(0) * rows).astype(jnp.float32)
    lane = lax.broadcasted_iota(jnp.int32, (rows, LANES), 1)
    ang = pos * inv_ref[...]
    cos_ref[...] = jnp.cos(ang)
    sin_ref[...] = jnp.where(lane < LANES // 2, -1.0, 1.0) * jnp.sin(ang)


def _rope_tables(seq):
    half = RET_HEAD_DIM // 2
    inv = 1.0 / (ROPE_BASE ** (jnp.arange(half, dtype=jnp.float32) / half))
    inv2 = jnp.concatenate([inv, inv])[None, :]
    rows = 512
    return pl.pallas_call(
        _rope_table_kernel,
        grid=(seq // rows,),
        in_specs=[pl.BlockSpec((1, LANES), lambda i: (0, 0))],
        out_specs=[pl.BlockSpec((rows, LANES), lambda i: (i, 0))] * 2,
        out_shape=[jax.ShapeDtypeStruct((seq, LANES), jnp.float32)] * 2,
        name="rope_tables",
    )(inv2)


def _input_kernel(x_ref, g_ref, w_ref, wf_ref, fb_ref, cos_ref, sin_ref,
                  fq_ref, fk_ref, fv_ref, fg_ref, rq_ref, rk_ref, rv_ref, rg_ref, c_ref,
                  carry_ref, *, seq_tiles):
    rows = x_ref.shape[0]
    x = x_ref[...]
    ms = jnp.mean(x * x, axis=-1, keepdims=True)
    u = (x * lax.rsqrt(ms + NORM_EPS) * g_ref[...]).astype(jnp.bfloat16)

    def silu(z):
        return z * (1.0 / (1.0 + jnp.exp(-z)))

    def rotary(z):
        cos = cos_ref[...]
        sin = sin_ref[...]
        heads = []
        for h in range(RET_HEADS):
            zh = z[:, h * LANES:(h + 1) * LANES]
            heads.append(zh * cos + pltpu.roll(zh, LANES // 2, axis=1) * sin)
        return jnp.concatenate(heads, axis=1)

    fq_ref[...] = (_dot(u, w_ref[0]) * (FOX_HEAD_DIM ** -0.5 * LOG2E)).astype(jnp.bfloat16)
    fk_ref[...] = _dot(u, w_ref[1]).astype(jnp.bfloat16)
    fv_ref[...] = _dot(u, w_ref[2]).astype(jnp.bfloat16)
    fg_ref[...] = silu(_dot(u, w_ref[3])).astype(jnp.bfloat16)
    rq_ref[...] = rotary(_dot(u, w_ref[4])).astype(jnp.bfloat16)
    rk_ref[...] = (rotary(_dot(u, w_ref[5])) * (RET_HEAD_DIM ** -0.5)).astype(jnp.bfloat16)
    rv_ref[...] = _dot(u, w_ref[6]).astype(jnp.bfloat16)
    rg_ref[...] = silu(_dot(u, w_ref[7])).astype(jnp.bfloat16)

    f = _dot(u, wf_ref[...]) + fb_ref[...]
    logf = -(jnp.maximum(-f, 0.0) + jnp.log1p(jnp.exp(-jnp.abs(f)))) * LOG2E
    y = jnp.transpose(logf)[:FOX_HEADS, :]
    lane = lax.broadcasted_iota(jnp.int32, y.shape, 1)
    shift = 1
    while shift < rows:
        y = y + jnp.where(lane >= shift, pltpu.roll(y, shift, axis=1), 0.0)
        shift *= 2

    @pl.when(pl.program_id(0) % seq_tiles == 0)
    def _():
        carry_ref[...] = jnp.zeros_like(carry_ref)

    y = y + carry_ref[:, 0:1]
    c_ref[0] = y
    carry_ref[...] = jnp.broadcast_to(y[:, rows - 1:rows], carry_ref.shape)


def _input_stage(x2, gain, w8, wf, fbias, cos_t, sin_t, batch, seq):
    n = x2.shape[0]
    rows = IN_ROWS
    seq_tiles = seq // rows
    act = jax.ShapeDtypeStruct((n, FOX_WIDTH), jnp.bfloat16)
    act_spec = pl.BlockSpec((rows, FOX_WIDTH), lambda i: (i, 0))
    tab_spec = pl.BlockSpec((rows, LANES), lambda i: (i % seq_tiles, 0))
    return pl.pallas_call(
        functools.partial(_input_kernel, seq_tiles=seq_tiles),
        grid=(n // rows,),
        in_specs=[
            pl.BlockSpec((rows, D_MODEL), lambda i: (i, 0)),
            pl.BlockSpec((1, D_MODEL), lambda i: (0, 0)),
            pl.BlockSpec((8, D_MODEL, FOX_WIDTH), lambda i: (0, 0, 0)),
            pl.BlockSpec((D_MODEL, LANES), lambda i: (0, 0)),
            pl.BlockSpec((1, LANES), lambda i: (0, 0)),
            tab_spec,
            tab_spec,
        ],
        out_specs=[act_spec] * 8 + [pl.BlockSpec((1, FOX_HEADS, rows), lambda i: (i // seq_tiles, 0, i % seq_tiles))],
        out_shape=[act] * 8 + [jax.ShapeDtypeStruct((batch, FOX_HEADS, seq), jnp.float32)],
        scratch_shapes=[pltpu.VMEM((FOX_HEADS, LANES), jnp.float32)],
        compiler_params=pltpu.CompilerParams(dimension_semantics=("arbitrary",), vmem_limit_bytes=VMEM_LIMIT),
        name="input_stage",
    )(x2, gain, w8, wf, fbias, cos_t, sin_t)


def _fox_kernel(q_ref, k_ref, v_ref, g_ref, c_ref, o_ref):
    blk = q_ref.shape[1]
    i = pl.program_id(2)
    q = q_ref[0]
    lane = lax.broadcasted_iota(jnp.int32, q.shape, 1)
    first = lane < FOX_HEAD_DIM
    zero = jnp.zeros_like(q)
    qs = (jnp.where(first, q, zero), jnp.where(first, zero, q))

    def block(j, state, diagonal):
        start = pl.multiple_of(j * blk, blk)
        k = k_ref[0, pl.ds(start, blk), :]
        v = v_ref[0, pl.ds(start, blk), :]
        new_state = []
        for h in range(2):
            m, l, acc = state[h]
            s = _dot_nt(qs[h], k) - c_ref[0, 0, h:h + 1, pl.ds(start, blk)]
            if diagonal:
                row = lax.broadcasted_iota(jnp.int32, s.shape, 0)
                col = lax.broadcasted_iota(jnp.int32, s.shape, 1)
                s = jnp.where(col <= row, s, NEG_BIG)
            m_new = jnp.maximum(m, jnp.max(s, axis=1, keepdims=True))
            p = jnp.exp2(s - m_new)
            alpha = jnp.exp2(m - m_new)
            l = alpha * l + jnp.sum(p, axis=1, keepdims=True)
            acc = alpha * acc + _dot(p.astype(jnp.bfloat16), v)
            new_state.append((m_new, l, acc))
        return tuple(new_state)

    init = tuple(
        (jnp.full((blk, 1), NEG_BIG, jnp.float32), jnp.zeros((blk, 1), jnp.float32), jnp.zeros((blk, LANES), jnp.float32))
        for _ in range(2)
    )
    state = lax.fori_loop(0, i, lambda j, st: block(j, st, False), init)
    (_, l0, acc0), (_, l1, acc1) = block(i, state, True)
    out = jnp.where(first, acc0 / l0, acc1 / l1)
    o_ref[0] = (out * g_ref[0].astype(jnp.float32)).astype(o_ref.dtype)


def _fox_stage(fq, fk, fv, fg, c4):
    batch, seq, _ = fq.shape
    blk = FOX_BLOCK
    pairs = FOX_WIDTH // LANES
    tile = pl.BlockSpec((1, blk, LANES), lambda b, p, i: (b, i, p))
    full = pl.BlockSpec((1, seq, LANES), lambda b, p, i: (b, 0, p))
    return pl.pallas_call(
        _fox_kernel,
        grid=(batch, pairs, seq // blk),
        in_specs=[tile, full, full, tile, pl.BlockSpec((1, 1, 2, seq), lambda b, p, i: (b, p, 0, 0))],
        out_specs=tile,
        out_shape=jax.ShapeDtypeStruct((batch, seq, FOX_WIDTH), jnp.bfloat16),
        compiler_params=pltpu.CompilerParams(
            dimension_semantics=("arbitrary", "arbitrary", "arbitrary"), vmem_limit_bytes=VMEM_LIMIT),
        name="fox_stage",
    )(fq, fk, fv, fg, c4)


def _ret_kernel(lg_ref, q_ref, k_ref, v_ref, g_ref, o_ref, state_ref, decay_ref):
    blk = q_ref.shape[1]
    lg = lg_ref[pl.program_id(1)]

    @pl.when(pl.program_id(2) == 0)
    def _():
        state_ref[...] = jnp.zeros_like(state_ref)
        row = lax.broadcasted_iota(jnp.int32, (blk, blk), 0)
        col = lax.broadcasted_iota(jnp.int32, (blk, blk), 1)
        dist = jnp.abs(row - col).astype(jnp.float32)
        decay_ref[...] = jnp.where(col // CHUNK <= row // CHUNK, jnp.exp(lg * dist), 0.0)

    q = q_ref[0]
    k = k_ref[0]
    v = v_ref[0]
    pos = lax.broadcasted_iota(jnp.int32, (blk, 1), 0).astype(jnp.float32)
    q_decay = jnp.exp(lg * (pos + 1.0))
    k_decay = jnp.exp(lg * (blk - 1.0 - pos))
    state = state_ref[...]

    scores = _dot_nt(q, k) * decay_ref[...]
    inner = _dot(scores.astype(jnp.bfloat16), v)
    cross = _dot(q, state.astype(jnp.bfloat16)) * q_decay
    kd = (k.astype(jnp.float32) * k_decay).astype(jnp.bfloat16)
    state_ref[...] = state * jnp.exp(lg * blk) + _dot_tn(kd, v)

    o = inner + cross
    mu = jnp.mean(o, axis=-1, keepdims=True)
    d = o - mu
    var = jnp.mean(d * d, axis=-1, keepdims=True)
    y = d * lax.rsqrt(var + GN_EPS)
    o_ref[0] = (y * g_ref[0].astype(jnp.float32)).astype(o_ref.dtype)


def _ret_stage(log_gamma, rq, rk, rv, rg):
    batch, seq, _ = rq.shape
    blk = RET_BLOCK
    tile = pl.BlockSpec((1, blk, LANES), lambda b, h, t: (b, t, h))
    return pl.pallas_call(
        _ret_kernel,
        grid=(batch, RET_HEADS, seq // blk),
        in_specs=[pl.BlockSpec(memory_space=pltpu.SMEM), tile, tile, tile, tile],
        out_specs=tile,
        out_shape=jax.ShapeDtypeStruct((batch, seq, RET_WIDTH), jnp.bfloat16),
        scratch_shapes=[pltpu.VMEM((RET_HEAD_DIM, RET_HEAD_DIM), jnp.float32), pltpu.VMEM((blk, blk), jnp.float32)],
        compiler_params=pltpu.CompilerParams(
            dimension_semantics=("arbitrary", "arbitrary", "arbitrary"), vmem_limit_bytes=VMEM_LIMIT),
        name="ret_stage",
    )(log_gamma, rq, rk, rv, rg)


def _output_kernel(mf_ref, mr_ref, wf_ref, wr_ref, g_ref, x_ref, o_ref):
    out = _dot(mf_ref[...], wf_ref[...]) + _dot(mr_ref[...], wr_ref[...])
    ms = jnp.mean(out * out, axis=-1, keepdims=True)
    o_ref[...] = x_ref[...] + out * lax.rsqrt(ms + NORM_EPS) * g_ref[...]


def _output_stage(mf, mr, wo_f, wo_r, gain, x2):
    n = x2.shape[0]
    rows = OUT_ROWS
    return pl.pallas_call(
        _output_kernel,
        grid=(n // rows,),
        in_specs=[
            pl.BlockSpec((rows, FOX_WIDTH), lambda i: (i, 0)),
            pl.BlockSpec((rows, RET_WIDTH), lambda i: (i, 0)),
            pl.BlockSpec((FOX_WIDTH, D_MODEL), lambda i: (0, 0)),
            pl.BlockSpec((RET_WIDTH, D_MODEL), lambda i: (0, 0)),
            pl.BlockSpec((1, D_MODEL), lambda i: (0, 0)),
            pl.BlockSpec((rows, D_MODEL), lambda i: (i, 0)),
        ],
        out_specs=pl.BlockSpec((rows, D_MODEL), lambda i: (i, 0)),
        out_shape=jax.ShapeDtypeStruct((n, D_MODEL), jnp.float32),
        compiler_params=pltpu.CompilerParams(dimension_semantics=("arbitrary",), vmem_limit_bytes=VMEM_LIMIT),
        name="output_stage",
    )(mf, mr, wo_f, wo_r, gain, x2)


def _layer(h, pre_gain, w_in, forget_bias, w_out, post_gain, cos_t, sin_t, log_gamma):
    batch, seq, _ = h.shape
    x2 = h.reshape(batch * seq, D_MODEL)
    wb = w_in.astype(jnp.bfloat16)
    fox_cols = wb[:, :4 * FOX_WIDTH].reshape(D_MODEL, 4, FOX_WIDTH)
    ret_cols = wb[:, 4 * FOX_WIDTH + FOX_HEADS:].reshape(D_MODEL, 4, RET_WIDTH)
    w8 = jnp.concatenate([fox_cols, ret_cols], axis=1).transpose(1, 0, 2)
    wf = jnp.pad(wb[:, 4 * FOX_WIDTH:4 * FOX_WIDTH + FOX_HEADS], ((0, 0), (0, LANES - FOX_HEADS)))
    fbias = jnp.pad(forget_bias.astype(jnp.float32), (0, LANES - FOX_HEADS))[None, :]

    fq, fk, fv, fg, rq, rk, rv, rg, c = _input_stage(
        x2, pre_gain[None, :], w8, wf, fbias, cos_t, sin_t, batch, seq)

    def seq3(a):
        return a.reshape(batch, seq, a.shape[-1])

    c4 = c.reshape(batch, FOX_HEADS // 2, 2, seq)
    mixed_fox = _fox_stage(seq3(fq), seq3(fk), seq3(fv), seq3(fg), c4)
    mixed_ret = _ret_stage(log_gamma, seq3(rq), seq3(rk), seq3(rv), seq3(rg))

    wo = w_out.astype(jnp.bfloat16)
    out = _output_stage(
        mixed_fox.reshape(batch * seq, FOX_WIDTH), mixed_ret.reshape(batch * seq, RET_WIDTH),
        wo[:FOX_WIDTH], wo[FOX_WIDTH:], post_gain[None, :], x2)
    return out.reshape(batch, seq, D_MODEL)


def kernel(x, pre_norm_gain, w_in, fox_forget_bias, w_out, post_norm_gain):
    seq = x.shape[1]
    cos_t, sin_t = _rope_tables(seq)
    log_gamma = jnp.log1p(
        -jnp.exp(jnp.linspace(math.log(1.0 / 32), math.log(1.0 / 512), RET_HEADS))).astype(jnp.float32)
    h = x
    for layer in range(pre_norm_gain.shape[0]):
        h = _layer(h, pre_norm_gain[layer], w_in[layer], fox_forget_bias[layer], w_out[layer],
                   post_norm_gain[layer], cos_t, sin_t, log_gamma)
    return h
```

```python
import functools
import math

import jax
import jax.numpy as jnp
from jax import lax
from jax.experimental import pallas as pl
from jax.experimental.pallas import tpu as pltpu

D_MODEL = 1024
CHUNK = 64
FOX_HEADS = 8
FOX_HEAD_DIM = 64
FOX_WIDTH = FOX_HEADS * FOX_HEAD_DIM
RET_HEADS = 4
RET_HEAD_DIM = 128
RET_WIDTH = RET_HEADS * RET_HEAD_DIM
ROPE_BASE = 10000.0
NORM_EPS = 1e-6
GN_EPS = 1e-5

LANES = 128
LOG2E = math.log2(math.e)
NEG_BIG = -1e30

IN_ROWS = 512
FOX_BLOCK = 512
FOX_KEYS = 256
RET_BLOCK = 256
OUT_ROWS = 512
VMEM_LIMIT = 48 * 1024 * 1024


def _dot(a, b):
    return jnp.dot(a, b, preferred_element_type=jnp.float32)


def _dot_nt(a, b):
    return lax.dot_general(a, b, (((1,), (1,)), ((), ())), preferred_element_type=jnp.float32)


def _dot_tn(a, b):
    return lax.dot_general(a, b, (((0,), (0,)), ((), ())), preferred_element_type=jnp.float32)


def _rope_table_kernel(inv_ref, cos_ref, sin_ref):
    rows = cos_ref.shape[0]
    pos = (lax.broadcasted_iota(jnp.int32, (rows, LANES), 0) + pl.program_id(0) * rows).astype(jnp.float32)
    lane = lax.broadcasted_iota(jnp.int32, (rows, LANES), 1)
    ang = pos * inv_ref[...]
    cos_ref[...] = jnp.cos(ang)
    sin_ref[...] = jnp.where(lane < LANES // 2, -1.0, 1.0) * jnp.sin(ang)


def _rope_tables(seq):
    half = RET_HEAD_DIM // 2
    inv = 1.0 / (ROPE_BASE ** (jnp.arange(half, dtype=jnp.float32) / half))
    inv2 = jnp.concatenate([inv, inv])[None, :]
    rows = 512
    return pl.pallas_call(
        _rope_table_kernel,
        grid=(seq // rows,),
        in_specs=[pl.BlockSpec((1, LANES), lambda i: (0, 0))],
        out_specs=[pl.BlockSpec((rows, LANES), lambda i: (i, 0))] * 2,
        out_shape=[jax.ShapeDtypeStruct((seq, LANES), jnp.float32)] * 2,
        name="rope_tables",
    )(inv2)


def _input_kernel(x_ref, g_ref, w_ref, wf_ref, fb_ref, cos_ref, sin_ref,
                  fq_ref, fk_ref, fv_ref, fg_ref, rq_ref, rk_ref, rv_ref, rg_ref,
                  carry_ref, *, seq_tiles):
    rows = x_ref.shape[0]
    x = x_ref[...]
    ms = jnp.mean(x * x, axis=-1, keepdims=True)
    u = (x * lax.rsqrt(ms + NORM_EPS) * g_ref[...]).astype(jnp.bfloat16)

    def silu(z):
        return z * (1.0 / (1.0 + jnp.exp(-z)))

    def rotary(z):
        cos = cos_ref[...]
        sin = sin_ref[...]
        heads = []
        for h in range(RET_HEADS):
            zh = z[:, h * LANES:(h + 1) * LANES]
            heads.append(zh * cos + pltpu.roll(zh, LANES // 2, axis=1) * sin)
        return jnp.concatenate(heads, axis=1)

    fq_ref[...] = (_dot(u, w_ref[0]) * (FOX_HEAD_DIM ** -0.5 * LOG2E)).astype(jnp.bfloat16)
    fk = _dot(u, w_ref[1]).astype(jnp.bfloat16)
    fv_ref[...] = _dot(u, w_ref[2]).astype(jnp.bfloat16)
    fg_ref[...] = silu(_dot(u, w_ref[3])).astype(jnp.bfloat16)
    rq_ref[...] = rotary(_dot(u, w_ref[4])).astype(jnp.bfloat16)
    rk_ref[...] = (rotary(_dot(u, w_ref[5])) * (RET_HEAD_DIM ** -0.5)).astype(jnp.bfloat16)
    rv_ref[...] = _dot(u, w_ref[6]).astype(jnp.bfloat16)
    rg_ref[...] = silu(_dot(u, w_ref[7])).astype(jnp.bfloat16)

    f = _dot(u, wf_ref[...]) + fb_ref[...]
    y = -(jnp.maximum(-f, 0.0) + jnp.log1p(jnp.exp(-jnp.abs(f)))) * LOG2E
    row = lax.broadcasted_iota(jnp.int32, y.shape, 0)
    lane = lax.broadcasted_iota(jnp.int32, y.shape, 1)
    shift = 1
    while shift < rows:
        y = y + jnp.where(row >= shift, pltpu.roll(y, shift, axis=0), 0.0)
        shift *= 2

    @pl.when(pl.program_id(0) % seq_tiles == 0)
    def _():
        carry_ref[...] = jnp.zeros_like(carry_ref)

    y = y + carry_ref[0:1, :]
    carry_ref[...] = jnp.broadcast_to(y[rows - 1:rows, :], carry_ref.shape)

    neg_c = -y
    hi = neg_c.astype(jnp.bfloat16).astype(jnp.float32)
    rest = neg_c - hi
    mid = rest.astype(jnp.bfloat16).astype(jnp.float32)
    lo = rest - mid
    piece = jnp.where(lane % 3 == 0, hi, jnp.where(lane % 3 == 1, mid, lo))
    pair_lanes = 2 * 3
    for p in range(FOX_HEADS // 2):
        shifted = piece if p == 0 else pltpu.roll(piece, LANES - pair_lanes * p, axis=1)
        extra = jnp.where(lane < pair_lanes, shifted, 0.0).astype(jnp.bfloat16)
        fk_ref[:, 2 * p * LANES:(2 * p + 1) * LANES] = fk[:, p * LANES:(p + 1) * LANES]
        fk_ref[:, (2 * p + 1) * LANES:(2 * p + 2) * LANES] = extra


def _input_stage(x2, gain, w8, wf, fbias, cos_t, sin_t, seq):
    n = x2.shape[0]
    rows = IN_ROWS
    seq_tiles = seq // rows
    act = jax.ShapeDtypeStruct((n, FOX_WIDTH), jnp.bfloat16)
    act_spec = pl.BlockSpec((rows, FOX_WIDTH), lambda i: (i, 0))
    wide = jax.ShapeDtypeStruct((n, 2 * FOX_WIDTH), jnp.bfloat16)
    wide_spec = pl.BlockSpec((rows, 2 * FOX_WIDTH), lambda i: (i, 0))
    tab_spec = pl.BlockSpec((rows, LANES), lambda i: (i % seq_tiles, 0))
    return pl.pallas_call(
        functools.partial(_input_kernel, seq_tiles=seq_tiles),
        grid=(n // rows,),
        in_specs=[
            pl.BlockSpec((rows, D_MODEL), lambda i: (i, 0)),
            pl.BlockSpec((1, D_MODEL), lambda i: (0, 0)),
            pl.BlockSpec((8, D_MODEL, FOX_WIDTH), lambda i: (0, 0, 0)),
            pl.BlockSpec((D_MODEL, LANES), lambda i: (0, 0)),
            pl.BlockSpec((1, LANES), lambda i: (0, 0)),
            tab_spec,
            tab_spec,
        ],
        out_specs=[act_spec, wide_spec] + [act_spec] * 6,
        out_shape=[act, wide] + [act] * 6,
        scratch_shapes=[pltpu.VMEM((8, LANES), jnp.float32)],
        compiler_params=pltpu.CompilerParams(dimension_semantics=("arbitrary",), vmem_limit_bytes=VMEM_LIMIT),
        name="input_stage",
    )(x2, gain, w8, wf, fbias, cos_t, sin_t)


ONES_ROWS = 16


def _fox_kernel(q_ref, k_ref, v_ref, g_ref, o_ref, vt_ref):
    blk = q_ref.shape[1]
    i = pl.program_id(2)
    dh = FOX_HEAD_DIM

    sub = vt_ref.shape[3]
    subs = blk // sub

    @pl.when(i == 0)
    def _():
        ones = jnp.ones((ONES_ROWS, sub), jnp.bfloat16)
        for jb in range(v_ref.shape[1] // sub):
            vt = jnp.transpose(v_ref[0, jb * sub:(jb + 1) * sub, :].astype(jnp.float32))
            for h in range(2):
                vt_ref[h, jb, 0:dh, :] = vt[h * dh:(h + 1) * dh].astype(jnp.bfloat16)
                vt_ref[h, jb, dh:dh + ONES_ROWS, :] = ones

    qt = jnp.transpose(q_ref[0].astype(jnp.float32))
    row = lax.broadcasted_iota(jnp.int32, qt.shape, 0)
    qts = []
    for h in range(2):
        own = jnp.where((row >= h * dh) & (row < (h + 1) * dh), qt, 0.0)
        ones = jnp.where((row >= 3 * h) & (row < 3 * h + 3), 1.0, 0.0)
        qts.append(jnp.concatenate([own, ones], axis=0).astype(jnp.bfloat16))

    def block(j, state, diagonal):
        state = list(state)
        units = [(t, h) for t in range(subs) for h in range(2)]

        def scores(t, h):
            start = pl.multiple_of(j * blk + t * sub, sub)
            s = _dot(k_ref[0, pl.ds(start, sub), :], qts[h])
            if diagonal:
                key = lax.broadcasted_iota(jnp.int32, s.shape, 0) + t * sub
                query = lax.broadcasted_iota(jnp.int32, s.shape, 1)
                s = jnp.where(key <= query, s, NEG_BIG)
            return s

        ahead = 2
        pending = [scores(*u) for u in units[:ahead]]
        for n, (t, h) in enumerate(units):
            s = pending.pop(0)
            m, acc = state[h]
            m_new = jnp.maximum(m, jnp.max(s, axis=0, keepdims=True))
            if n + ahead < len(units):
                pending.append(scores(*units[n + ahead]))
            p = jnp.exp2(s - m_new).astype(jnp.bfloat16)
            acc = jnp.exp2(m - m_new) * acc + _dot(vt_ref[h, j * subs + t], p)
            state[h] = (m_new, acc)
        return tuple(state)

    init = tuple(
        (jnp.full((1, blk), NEG_BIG, jnp.float32), jnp.zeros((dh + ONES_ROWS, blk), jnp.float32)) for _ in range(2))
    state = lax.fori_loop(0, i, lambda j, st: block(j, st, False), init)
    (_, acc0), (_, acc1) = block(i, state, True)
    out_t = jnp.concatenate([acc0[0:dh] / acc0[dh:dh + 1], acc1[0:dh] / acc1[dh:dh + 1]], axis=0)
    o_ref[0] = (jnp.transpose(out_t) * g_ref[0].astype(jnp.float32)).astype(o_ref.dtype)


def _fox_stage(fq, fk, fv, fg):
    batch, seq, _ = fq.shape
    blk = FOX_BLOCK
    pairs = FOX_WIDTH // LANES
    tile = pl.BlockSpec((1, blk, LANES), lambda b, p, i: (b, i, p))
    return pl.pallas_call(
        _fox_kernel,
        grid=(batch, pairs, seq // blk),
        in_specs=[
            tile,
            pl.BlockSpec((1, seq, 2 * LANES), lambda b, p, i: (b, 0, p)),
            pl.BlockSpec((1, seq, LANES), lambda b, p, i: (b, 0, p)),
            tile,
        ],
        out_specs=tile,
        out_shape=jax.ShapeDtypeStruct((batch, seq, FOX_WIDTH), jnp.bfloat16),
        scratch_shapes=[pltpu.VMEM((2, seq // FOX_KEYS, FOX_HEAD_DIM + ONES_ROWS, FOX_KEYS), jnp.bfloat16)],
        compiler_params=pltpu.CompilerParams(
            dimension_semantics=("arbitrary", "arbitrary", "arbitrary"), vmem_limit_bytes=VMEM_LIMIT),
        name="fox_stage",
    )(fq, fk, fv, fg)


def _ret_kernel(lg_ref, q_ref, k_ref, v_ref, g_ref, o_ref, state_ref, decay_ref, qdec_ref, kdec_ref, sdec_ref):
    blk = q_ref.shape[1]

    @pl.when((pl.program_id(0) == 0) & (pl.program_id(1) == 0))
    def _():
        row = lax.broadcasted_iota(jnp.int32, (blk, blk), 0)
        col = lax.broadcasted_iota(jnp.int32, (blk, blk), 1)
        dist = jnp.abs(row - col).astype(jnp.float32)
        pos = lax.broadcasted_iota(jnp.int32, (blk, LANES), 0).astype(jnp.float32)
        for h in range(RET_HEADS):
            lg = lg_ref[h]
            decay_ref[h] = jnp.where(col // CHUNK <= row // CHUNK, jnp.exp(lg * dist), 0.0)
            qdec_ref[h] = jnp.exp(lg * (pos + 1.0))
            kdec_ref[h] = jnp.exp(lg * (blk - 1.0 - pos))
            sdec_ref[h] = jnp.exp(lg * jnp.full((8, LANES), blk, jnp.float32))

    @pl.when(pl.program_id(1) == 0)
    def _():
        state_ref[...] = jnp.zeros_like(state_ref)

    for h in range(RET_HEADS):
        cols = slice(h * LANES, (h + 1) * LANES)
        q = q_ref[0, :, cols]
        k = k_ref[0, :, cols]
        v = v_ref[0, :, cols]
        state = state_ref[h]

        scores = _dot_nt(q, k) * decay_ref[h]
        inner = _dot(scores.astype(jnp.bfloat16), v)
        cross = _dot(q, state.astype(jnp.bfloat16)) * qdec_ref[h]
        kd = (k.astype(jnp.float32) * kdec_ref[h]).astype(jnp.bfloat16)
        state_ref[h] = state * sdec_ref[h, 0:1, :] + _dot_tn(kd, v)

        o = inner + cross
        mu = jnp.mean(o, axis=-1, keepdims=True)
        d = o - mu
        var = jnp.mean(d * d, axis=-1, keepdims=True)
        y = d * lax.rsqrt(var + GN_EPS)
        o_ref[0, :, cols] = (y * g_ref[0, :, cols].astype(jnp.float32)).astype(o_ref.dtype)


def _ret_stage(log_gamma, rq, rk, rv, rg):
    batch, seq, _ = rq.shape
    blk = RET_BLOCK
    tile = pl.BlockSpec((1, blk, RET_WIDTH), lambda b, t: (b, t, 0))
    return pl.pallas_call(
        _ret_kernel,
        grid=(batch, seq // blk),
        in_specs=[pl.BlockSpec(memory_space=pltpu.SMEM), tile, tile, tile, tile],
        out_specs=tile,
        out_shape=jax.ShapeDtypeStruct((batch, seq, RET_WIDTH), jnp.bfloat16),
        scratch_shapes=[
            pltpu.VMEM((RET_HEADS, RET_HEAD_DIM, RET_HEAD_DIM), jnp.float32),
            pltpu.VMEM((RET_HEADS, blk, blk), jnp.float32),
            pltpu.VMEM((RET_HEADS, blk, LANES), jnp.float32),
            pltpu.VMEM((RET_HEADS, blk, LANES), jnp.float32),
            pltpu.VMEM((RET_HEADS, 8, LANES), jnp.float32),
        ],
        compiler_params=pltpu.CompilerParams(
            dimension_semantics=("arbitrary", "arbitrary"), vmem_limit_bytes=VMEM_LIMIT),
        name="ret_stage",
    )(log_gamma, rq, rk, rv, rg)


def _output_kernel(mf_ref, mr_ref, wf_ref, wr_ref, g_ref, x_ref, o_ref):
    out = _dot(mf_ref[...], wf_ref[...]) + _dot(mr_ref[...], wr_ref[...])
    ms = jnp.mean(out * out, axis=-1, keepdims=True)
    o_ref[...] = x_ref[...] + out * lax.rsqrt(ms + NORM_EPS) * g_ref[...]


def _output_stage(mf, mr, wo_f, wo_r, gain, x2):
    n = x2.shape[0]
    rows = OUT_ROWS
    return pl.pallas_call(
        _output_kernel,
        grid=(n // rows,),
        in_specs=[
            pl.BlockSpec((rows, FOX_WIDTH), lambda i: (i, 0)),
            pl.BlockSpec((rows, RET_WIDTH), lambda i: (i, 0)),
            pl.BlockSpec((FOX_WIDTH, D_MODEL), lambda i: (0, 0)),
            pl.BlockSpec((RET_WIDTH, D_MODEL), lambda i: (0, 0)),
            pl.BlockSpec((1, D_MODEL), lambda i: (0, 0)),
            pl.BlockSpec((rows, D_MODEL), lambda i: (i, 0)),
        ],
        out_specs=pl.BlockSpec((rows, D_MODEL), lambda i: (i, 0)),
        out_shape=jax.ShapeDtypeStruct((n, D_MODEL), jnp.float32),
        compiler_params=pltpu.CompilerParams(dimension_semantics=("arbitrary",), vmem_limit_bytes=VMEM_LIMIT),
        name="output_stage",
    )(mf, mr, wo_f, wo_r, gain, x2)


def _layer(h, pre_gain, w_in, forget_bias, w_out, post_gain, cos_t, sin_t, log_gamma):
    batch, seq, _ = h.shape
    x2 = h.reshape(batch * seq, D_MODEL)
    wb = w_in.astype(jnp.bfloat16)
    fox_cols = wb[:, :4 * FOX_WIDTH].reshape(D_MODEL, 4, FOX_WIDTH)
    ret_cols = wb[:, 4 * FOX_WIDTH + FOX_HEADS:].reshape(D_MODEL, 4, RET_WIDTH)
    w8 = jnp.concatenate([fox_cols, ret_cols], axis=1).transpose(1, 0, 2)
    wf = jnp.pad(jnp.repeat(wb[:, 4 * FOX_WIDTH:4 * FOX_WIDTH + FOX_HEADS], 3, axis=1),
                 ((0, 0), (0, LANES - 3 * FOX_HEADS)))
    fbias = jnp.pad(jnp.repeat(forget_bias.astype(jnp.float32), 3), (0, LANES - 3 * FOX_HEADS))[None, :]

    fq, fk, fv, fg, rq, rk, rv, rg = _input_stage(
        x2, pre_gain[None, :], w8, wf, fbias, cos_t, sin_t, seq)

    def seq3(a):
        return a.reshape(batch, seq, a.shape[-1])

    mixed_fox = _fox_stage(seq3(fq), seq3(fk), seq3(fv), seq3(fg))
    mixed_ret = _ret_stage(log_gamma, seq3(rq), seq3(rk), seq3(rv), seq3(rg))

    wo = w_out.astype(jnp.bfloat16)
    out = _output_stage(
        mixed_fox.reshape(batch * seq, FOX_WIDTH), mixed_ret.reshape(batch * seq, RET_WIDTH),
        wo[:FOX_WIDTH], wo[FOX_WIDTH:], post_gain[None, :], x2)
    return out.reshape(batch, seq, D_MODEL)


def kernel(x, pre_norm_gain, w_in, fox_forget_bias, w_out, post_norm_gain):
    seq = x.shape[1]
    cos_t, sin_t = _rope_tables(seq)
    log_gamma = jnp.log1p(
        -jnp.exp(jnp.linspace(math.log(1.0 / 32), math.log(1.0 / 512), RET_HEADS))).astype(jnp.float32)
    h = x
    for layer in range(pre_norm_gain.shape[0]):
        h = _layer(h, pre_norm_gain[layer], w_in[layer], fox_forget_bias[layer], w_out[layer],
                   post_norm_gain[layer], cos_t, sin_t, log_gamma)
    return h
```

```python
import functools
import math

import jax
import jax.numpy as jnp
from jax import lax
from jax.experimental import pallas as pl
from jax.experimental.pallas import tpu as pltpu

D_MODEL = 1024
CHUNK = 64
FOX_HEADS = 8
FOX_HEAD_DIM = 64
FOX_WIDTH = FOX_HEADS * FOX_HEAD_DIM
RET_HEADS = 4
RET_HEAD_DIM = 128
RET_WIDTH = RET_HEADS * RET_HEAD_DIM
ROPE_BASE = 10000.0
NORM_EPS = 1e-6
GN_EPS = 1e-5

LANES = 128
LOG2E = math.log2(math.e)
NEG_BIG = -1e30

IN_ROWS = 512
FOX_BLOCK = 512
FOX_KEYS = 256
RET_BLOCK = 256
OUT_ROWS = 512
VMEM_LIMIT = 48 * 1024 * 1024


def _dot(a, b):
    return jnp.dot(a, b, preferred_element_type=jnp.float32)


def _dot_nt(a, b):
    return lax.dot_general(a, b, (((1,), (1,)), ((), ())), preferred_element_type=jnp.float32)


def _dot_tn(a, b):
    return lax.dot_general(a, b, (((0,), (0,)), ((), ())), preferred_element_type=jnp.float32)


def _rope_table_kernel(inv_ref, cos_ref, sin_ref):
    rows = cos_ref.shape[0]
    pos = (lax.broadcasted_iota(jnp.int32, (rows, LANES), 0) + pl.program_id(0) * rows).astype(jnp.float32)
    lane = lax.broadcasted_iota(jnp.int32, (rows, LANES), 1)
    ang = pos * inv_ref[...]
    cos_ref[...] = jnp.cos(ang)
    sin_ref[...] = jnp.where(lane < LANES // 2, -1.0, 1.0) * jnp.sin(ang)


def _rope_tables(seq):
    half = RET_HEAD_DIM // 2
    inv = 1.0 / (ROPE_BASE ** (jnp.arange(half, dtype=jnp.float32) / half))
    inv2 = jnp.concatenate([inv, inv])[None, :]
    rows = 512
    return pl.pallas_call(
        _rope_table_kernel,
        grid=(seq // rows,),
        in_specs=[pl.BlockSpec((1, LANES), lambda i: (0, 0))],
        out_specs=[pl.BlockSpec((rows, LANES), lambda i: (i, 0))] * 2,
        out_shape=[jax.ShapeDtypeStruct((seq, LANES), jnp.float32)] * 2,
        name="rope_tables",
    )(inv2)


def _input_kernel(x_ref, g_ref, w_ref, wf_ref, fb_ref, cos_ref, sin_ref,
                  fq_ref, fk_ref, fv_ref, fg_ref, rq_ref, rk_ref, rv_ref, rg_ref,
                  carry_ref, *, seq_tiles):
    rows = x_ref.shape[0]
    x = x_ref[...]
    ms = jnp.mean(x * x, axis=-1, keepdims=True)
    u = (x * lax.rsqrt(ms + NORM_EPS) * g_ref[...]).astype(jnp.bfloat16)

    def silu(z):
        return z * (1.0 / (1.0 + jnp.exp(-z)))

    def rotary(z):
        cos = cos_ref[...]
        sin = sin_ref[...]
        heads = []
        for h in range(RET_HEADS):
            zh = z[:, h * LANES:(h + 1) * LANES]
            heads.append(zh * cos + pltpu.roll(zh, LANES // 2, axis=1) * sin)
        return jnp.concatenate(heads, axis=1)

    f = _dot(u, wf_ref[...]) + fb_ref[...]
    y = -(jnp.maximum(-f, 0.0) + jnp.log1p(jnp.exp(-jnp.abs(f)))) * LOG2E
    row = lax.broadcasted_iota(jnp.int32, y.shape, 0)
    lane = lax.broadcasted_iota(jnp.int32, y.shape, 1)
    shift = 1
    while shift < rows:
        y = y + jnp.where(row >= shift, pltpu.roll(y, shift, axis=0), 0.0)
        shift *= 2

    @pl.when(pl.program_id(0) % seq_tiles == 0)
    def _():
        carry_ref[...] = jnp.zeros_like(carry_ref)

    y = y + carry_ref[0:1, :]
    carry_ref[...] = jnp.broadcast_to(y[rows - 1:rows, :], carry_ref.shape)

    neg_c = -y
    hi = neg_c.astype(jnp.bfloat16).astype(jnp.float32)
    rest = neg_c - hi
    mid = rest.astype(jnp.bfloat16).astype(jnp.float32)
    lo = rest - mid
    piece = jnp.where(lane % 3 == 0, hi, jnp.where(lane % 3 == 1, mid, lo))
    pair_lanes = 2 * 3
    for p in range(FOX_HEADS // 2):
        shifted = piece if p == 0 else pltpu.roll(piece, LANES - pair_lanes * p, axis=1)
        extra = jnp.where(lane < pair_lanes, shifted, 0.0).astype(jnp.bfloat16)
        fk_ref[:, (2 * p + 1) * LANES:(2 * p + 2) * LANES] = extra

    fq_ref[...] = (_dot(u, w_ref[0]) * (FOX_HEAD_DIM ** -0.5 * LOG2E)).astype(jnp.bfloat16)
    fk = _dot(u, w_ref[1]).astype(jnp.bfloat16)
    for p in range(FOX_HEADS // 2):
        fk_ref[:, 2 * p * LANES:(2 * p + 1) * LANES] = fk[:, p * LANES:(p + 1) * LANES]
    fv_ref[...] = _dot(u, w_ref[2]).astype(jnp.bfloat16)
    fg_ref[...] = silu(_dot(u, w_ref[3])).astype(jnp.bfloat16)
    rq_ref[...] = rotary(_dot(u, w_ref[4])).astype(jnp.bfloat16)
    rk_ref[...] = (rotary(_dot(u, w_ref[5])) * (RET_HEAD_DIM ** -0.5)).astype(jnp.bfloat16)
    rv_ref[...] = _dot(u, w_ref[6]).astype(jnp.bfloat16)
    rg_ref[...] = silu(_dot(u, w_ref[7])).astype(jnp.bfloat16)


def _input_stage(x2, gain, w8, wf, fbias, cos_t, sin_t, seq):
    n = x2.shape[0]
    rows = IN_ROWS
    seq_tiles = seq // rows
    act = jax.ShapeDtypeStruct((n, FOX_WIDTH), jnp.bfloat16)
    act_spec = pl.BlockSpec((rows, FOX_WIDTH), lambda i: (i, 0))
    wide = jax.ShapeDtypeStruct((n, 2 * FOX_WIDTH), jnp.bfloat16)
    wide_spec = pl.BlockSpec((rows, 2 * FOX_WIDTH), lambda i: (i, 0))
    tab_spec = pl.BlockSpec((rows, LANES), lambda i: (i % seq_tiles, 0))
    return pl.pallas_call(
        functools.partial(_input_kernel, seq_tiles=seq_tiles),
        grid=(n // rows,),
        in_specs=[
            pl.BlockSpec((rows, D_MODEL), lambda i: (i, 0)),
            pl.BlockSpec((1, D_MODEL), lambda i: (0, 0)),
            pl.BlockSpec((8, D_MODEL, FOX_WIDTH), lambda i: (0, 0, 0)),
            pl.BlockSpec((D_MODEL, LANES), lambda i: (0, 0)),
            pl.BlockSpec((1, LANES), lambda i: (0, 0)),
            tab_spec,
            tab_spec,
        ],
        out_specs=[act_spec, wide_spec] + [act_spec] * 6,
        out_shape=[act, wide] + [act] * 6,
        scratch_shapes=[pltpu.VMEM((8, LANES), jnp.float32)],
        compiler_params=pltpu.CompilerParams(dimension_semantics=("arbitrary",), vmem_limit_bytes=VMEM_LIMIT),
        name="input_stage",
    )(x2, gain, w8, wf, fbias, cos_t, sin_t)


ONES_ROWS = 16


def _fox_kernel(q_ref, k_ref, v_ref, g_ref, o_ref, vt_ref, s_ref):
    blk = q_ref.shape[1]
    i = pl.program_id(2)
    dh = FOX_HEAD_DIM

    sub = vt_ref.shape[3]
    subs = blk // sub

    @pl.when(i == 0)
    def _():
        ones = jnp.ones((ONES_ROWS, sub), jnp.bfloat16)
        for jb in range(v_ref.shape[1] // sub):
            vt = jnp.transpose(v_ref[0, jb * sub:(jb + 1) * sub, :].astype(jnp.float32))
            for h in range(2):
                vt_ref[h, jb, 0:dh, :] = vt[h * dh:(h + 1) * dh].astype(jnp.bfloat16)
                vt_ref[h, jb, dh:dh + ONES_ROWS, :] = ones

    qt = jnp.transpose(q_ref[0].astype(jnp.float32))
    row = lax.broadcasted_iota(jnp.int32, qt.shape, 0)
    qts = []
    for h in range(2):
        own = jnp.where((row >= h * dh) & (row < (h + 1) * dh), qt, 0.0)
        ones = jnp.where((row >= 3 * h) & (row < 3 * h + 3), 1.0, 0.0)
        qts.append(jnp.concatenate([own, ones], axis=0).astype(jnp.bfloat16))

    half = blk // 2

    def causal(s, t, lo):
        key = lax.broadcasted_iota(jnp.int32, s.shape, 0) + t * sub
        query = lax.broadcasted_iota(jnp.int32, s.shape, 1) + lo
        return jnp.where(key <= query, s, NEG_BIG)

    def produce(j, t, h, lo=0, diagonal=False):
        start = pl.multiple_of(j * blk + t * sub, sub)
        s = _dot(k_ref[0, pl.ds(start, sub), :], qts[h][:, lo:])
        if diagonal:
            s = causal(s, t, lo)
        s_ref[2 * t + h, :, lo:] = s
        return jnp.max(s, axis=0, keepdims=True)

    def consume(state, j, t, h, col_max, lo=0, s=None):
        m, acc = state[h]
        m_new = jnp.maximum(m[:, lo:], col_max)
        if s is None:
            s = s_ref[2 * t + h, :, lo:]
        p = jnp.exp2(s - m_new).astype(jnp.bfloat16)
        acc_new = jnp.exp2(m[:, lo:] - m_new) * acc[:, lo:] + _dot(vt_ref[h, j * subs + t], p)
        if lo:
            m_new = jnp.concatenate([m[:, :lo], m_new], axis=1)
            acc_new = jnp.concatenate([acc[:, :lo], acc_new], axis=1)
        state[h] = (m_new, acc_new)

    def full_block(j, carry):
        state, (max0, max1) = carry
        state = list(state)
        max2 = produce(j, 1, 0)
        consume(state, j, 0, 0, max0)
        max3 = produce(j, 1, 1)
        consume(state, j, 0, 1, max1)
        max0 = produce(j + 1, 0, 0)
        consume(state, j, 1, 0, max2)
        max1 = produce(j + 1, 0, 1)
        consume(state, j, 1, 1, max3)
        return tuple(state), (max0, max1)

    init = tuple(
        (jnp.full((1, blk), NEG_BIG, jnp.float32), jnp.zeros((dh + ONES_ROWS, blk), jnp.float32)) for _ in range(2))
    state, _ = lax.fori_loop(0, i, full_block, (init, (produce(0, 0, 0), produce(0, 0, 1))))

    state = list(state)
    max2 = produce(i, 1, 0, lo=half, diagonal=True)
    for h in range(2):
        s = causal(s_ref[h], 0, 0)
        consume(state, i, 0, h, jnp.max(s, axis=0, keepdims=True), s=s)
        if h == 0:
            max3 = produce(i, 1, 1, lo=half, diagonal=True)
    consume(state, i, 1, 0, max2, lo=half)
    consume(state, i, 1, 1, max3, lo=half)
    (_, acc0), (_, acc1) = state
    out_t = jnp.concatenate([acc0[0:dh] / acc0[dh:dh + 1], acc1[0:dh] / acc1[dh:dh + 1]], axis=0)
    o_ref[0] = (jnp.transpose(out_t) * g_ref[0].astype(jnp.float32)).astype(o_ref.dtype)


def _fox_stage(fq, fk, fv, fg):
    batch, seq, _ = fq.shape
    blk = FOX_BLOCK
    pairs = FOX_WIDTH // LANES
    tile = pl.BlockSpec((1, blk, LANES), lambda b, p, i: (b, i, p))
    return pl.pallas_call(
        _fox_kernel,
        grid=(batch, pairs, seq // blk),
        in_specs=[
            tile,
            pl.BlockSpec((1, seq, 2 * LANES), lambda b, p, i: (b, 0, p)),
            pl.BlockSpec((1, seq, LANES), lambda b, p, i: (b, 0, p)),
            tile,
        ],
        out_specs=tile,
        out_shape=jax.ShapeDtypeStruct((batch, seq, FOX_WIDTH), jnp.bfloat16),
        scratch_shapes=[
            pltpu.VMEM((2, seq // FOX_KEYS, FOX_HEAD_DIM + ONES_ROWS, FOX_KEYS), jnp.bfloat16),
            pltpu.VMEM((2 * blk // FOX_KEYS, FOX_KEYS, blk), jnp.float32),
        ],
        compiler_params=pltpu.CompilerParams(
            dimension_semantics=("arbitrary", "arbitrary", "arbitrary"), vmem_limit_bytes=VMEM_LIMIT),
        name="fox_stage",
    )(fq, fk, fv, fg)


def _ret_kernel(lg_ref, q_ref, k_ref, v_ref, g_ref, o_ref, state_ref, decay_ref, qdec_ref, kdec_ref, sdec_ref):
    blk = q_ref.shape[1]

    @pl.when((pl.program_id(0) == 0) & (pl.program_id(1) == 0))
    def _():
        row = lax.broadcasted_iota(jnp.int32, (blk, blk), 0)
        col = lax.broadcasted_iota(jnp.int32, (blk, blk), 1)
        dist = jnp.abs(row - col).astype(jnp.float32)
        pos = lax.broadcasted_iota(jnp.int32, (blk, LANES), 0).astype(jnp.float32)
        for h in range(RET_HEADS):
            lg = lg_ref[h]
            decay_ref[h] = jnp.where(col // CHUNK <= row // CHUNK, jnp.exp(lg * dist), 0.0)
            qdec_ref[h] = jnp.exp(lg * (pos + 1.0))
            kdec_ref[h] = jnp.exp(lg * (blk - 1.0 - pos))
            sdec_ref[h] = jnp.exp(lg * jnp.full((8, LANES), blk, jnp.float32))

    @pl.when(pl.program_id(1) == 0)
    def _():
        state_ref[...] = jnp.zeros_like(state_ref)

    for h in range(RET_HEADS):
        cols = slice(h * LANES, (h + 1) * LANES)
        q = q_ref[0, :, cols]
        k = k_ref[0, :, cols]
        v = v_ref[0, :, cols]
        state = state_ref[h]

        scores = _dot_nt(q, k) * decay_ref[h]
        inner = _dot(scores.astype(jnp.bfloat16), v)
        cross = _dot(q, state.astype(jnp.bfloat16)) * qdec_ref[h]
        kd = (k.astype(jnp.float32) * kdec_ref[h]).astype(jnp.bfloat16)
        state_ref[h] = state * sdec_ref[h, 0:1, :] + _dot_tn(kd, v)

        o = inner + cross
        mu = jnp.mean(o, axis=-1, keepdims=True)
        d = o - mu
        var = jnp.mean(d * d, axis=-1, keepdims=True)
        y = d * lax.rsqrt(var + GN_EPS)
        o_ref[0, :, cols] = (y * g_ref[0, :, cols].astype(jnp.float32)).astype(o_ref.dtype)


def _ret_stage(log_gamma, rq, rk, rv, rg):
    batch, seq, _ = rq.shape
    blk = RET_BLOCK
    tile = pl.BlockSpec((1, blk, RET_WIDTH), lambda b, t: (b, t, 0))
    return pl.pallas_call(
        _ret_kernel,
        grid=(batch, seq // blk),
        in_specs=[pl.BlockSpec(memory_space=pltpu.SMEM), tile, tile, tile, tile],
        out_specs=tile,
        out_shape=jax.ShapeDtypeStruct((batch, seq, RET_WIDTH), jnp.bfloat16),
        scratch_shapes=[
            pltpu.VMEM((RET_HEADS, RET_HEAD_DIM, RET_HEAD_DIM), jnp.float32),
            pltpu.VMEM((RET_HEADS, blk, blk), jnp.float32),
            pltpu.VMEM((RET_HEADS, blk, LANES), jnp.float32),
            pltpu.VMEM((RET_HEADS, blk, LANES), jnp.float32),
            pltpu.VMEM((RET_HEADS, 8, LANES), jnp.float32),
        ],
        compiler_params=pltpu.CompilerParams(
            dimension_semantics=("arbitrary", "arbitrary"), vmem_limit_bytes=VMEM_LIMIT),
        name="ret_stage",
    )(log_gamma, rq, rk, rv, rg)


def _output_kernel(mf_ref, mr_ref, wf_ref, wr_ref, g_ref, x_ref, o_ref):
    out = _dot(mf_ref[...], wf_ref[...]) + _dot(mr_ref[...], wr_ref[...])
    ms = jnp.mean(out * out, axis=-1, keepdims=True)
    o_ref[...] = x_ref[...] + out * lax.rsqrt(ms + NORM_EPS) * g_ref[...]


def _output_stage(mf, mr, wo_f, wo_r, gain, x2):
    n = x2.shape[0]
    rows = OUT_ROWS
    return pl.pallas_call(
        _output_kernel,
        grid=(n // rows,),
        in_specs=[
            pl.BlockSpec((rows, FOX_WIDTH), lambda i: (i, 0)),
            pl.BlockSpec((rows, RET_WIDTH), lambda i: (i, 0)),
            pl.BlockSpec((FOX_WIDTH, D_MODEL), lambda i: (0, 0)),
            pl.BlockSpec((RET_WIDTH, D_MODEL), lambda i: (0, 0)),
            pl.BlockSpec((1, D_MODEL), lambda i: (0, 0)),
            pl.BlockSpec((rows, D_MODEL), lambda i: (i, 0)),
        ],
        out_specs=pl.BlockSpec((rows, D_MODEL), lambda i: (i, 0)),
        out_shape=jax.ShapeDtypeStruct((n, D_MODEL), jnp.float32),
        compiler_params=pltpu.CompilerParams(dimension_semantics=("arbitrary",), vmem_limit_bytes=VMEM_LIMIT),
        name="output_stage",
    )(mf, mr, wo_f, wo_r, gain, x2)


def _layer(h, pre_gain, w_in, forget_bias, w_out, post_gain, cos_t, sin_t, log_gamma):
    batch, seq, _ = h.shape
    x2 = h.reshape(batch * seq, D_MODEL)
    wb = w_in.astype(jnp.bfloat16)
    fox_cols = wb[:, :4 * FOX_WIDTH].reshape(D_MODEL, 4, FOX_WIDTH)
    ret_cols = wb[:, 4 * FOX_WIDTH + FOX_HEADS:].reshape(D_MODEL, 4, RET_WIDTH)
    w8 = jnp.concatenate([fox_cols, ret_cols], axis=1).transpose(1, 0, 2)
    wf = jnp.pad(jnp.repeat(wb[:, 4 * FOX_WIDTH:4 * FOX_WIDTH + FOX_HEADS], 3, axis=1),
                 ((0, 0), (0, LANES - 3 * FOX_HEADS)))
    fbias = jnp.pad(jnp.repeat(forget_bias.astype(jnp.float32), 3), (0, LANES - 3 * FOX_HEADS))[None, :]

    fq, fk, fv, fg, rq, rk, rv, rg = _input_stage(
        x2, pre_gain[None, :], w8, wf, fbias, cos_t, sin_t, seq)

    def seq3(a):
        return a.reshape(batch, seq, a.shape[-1])

    mixed_fox = _fox_stage(seq3(fq), seq3(fk), seq3(fv), seq3(fg))
    mixed_ret = _ret_stage(log_gamma, seq3(rq), seq3(rk), seq3(rv), seq3(rg))

    wo = w_out.astype(jnp.bfloat16)
    out = _output_stage(
        mixed_fox.reshape(batch * seq, FOX_WIDTH), mixed_ret.reshape(batch * seq, RET_WIDTH),
        wo[:FOX_WIDTH], wo[FOX_WIDTH:], post_gain[None, :], x2)
    return out.reshape(batch, seq, D_MODEL)


def kernel(x, pre_norm_gain, w_in, fox_forget_bias, w_out, post_norm_gain):
    seq = x.shape[1]
    cos_t, sin_t = _rope_tables(seq)
    log_gamma = jnp.log1p(
        -jnp.exp(jnp.linspace(math.log(1.0 / 32), math.log(1.0 / 512), RET_HEADS))).astype(jnp.float32)
    h = x
    for layer in range(pre_norm_gain.shape[0]):
        h = _layer(h, pre_norm_gain[layer], w_in[layer], fox_forget_bias[layer], w_out[layer],
                   post_norm_gain[layer], cos_t, sin_t, log_gamma)
    return h
```

```python
import functools
import math

import jax
import jax.numpy as jnp
from jax import lax
from jax.experimental import pallas as pl
from jax.experimental.pallas import tpu as pltpu

D_MODEL = 1024
CHUNK = 64
FOX_HEADS = 8
FOX_HEAD_DIM = 64
FOX_WIDTH = FOX_HEADS * FOX_HEAD_DIM
RET_HEADS = 4
RET_HEAD_DIM = 128
RET_WIDTH = RET_HEADS * RET_HEAD_DIM
ROPE_BASE = 10000.0
NORM_EPS = 1e-6
GN_EPS = 1e-5

LANES = 128
LOG2E = math.log2(math.e)
NEG_BIG = -1e30

IN_ROWS = 512
FOX_BLOCK = 512
FOX_KEYS = 256
RET_BLOCK = 256
OUT_ROWS = 512
VMEM_LIMIT = 48 * 1024 * 1024


def _dot(a, b):
    return jnp.dot(a, b, preferred_element_type=jnp.float32)


def _dot_nt(a, b):
    return lax.dot_general(a, b, (((1,), (1,)), ((), ())), preferred_element_type=jnp.float32)


def _dot_tn(a, b):
    return lax.dot_general(a, b, (((0,), (0,)), ((), ())), preferred_element_type=jnp.float32)


def _rope_table_kernel(inv_ref, cos_ref, sin_ref):
    rows = cos_ref.shape[0]
    pos = (lax.broadcasted_iota(jnp.int32, (rows, LANES), 0) + pl.program_id(0) * rows).astype(jnp.float32)
    lane = lax.broadcasted_iota(jnp.int32, (rows, LANES), 1)
    ang = pos * inv_ref[...]
    cos_ref[...] = jnp.cos(ang)
    sin_ref[...] = jnp.where(lane < LANES // 2, -1.0, 1.0) * jnp.sin(ang)


def _rope_tables(seq):
    half = RET_HEAD_DIM // 2
    inv = 1.0 / (ROPE_BASE ** (jnp.arange(half, dtype=jnp.float32) / half))
    inv2 = jnp.concatenate([inv, inv])[None, :]
    rows = 512
    return pl.pallas_call(
        _rope_table_kernel,
        grid=(seq // rows,),
        in_specs=[pl.BlockSpec((1, LANES), lambda i: (0, 0))],
        out_specs=[pl.BlockSpec((rows, LANES), lambda i: (i, 0))] * 2,
        out_shape=[jax.ShapeDtypeStruct((seq, LANES), jnp.float32)] * 2,
        name="rope_tables",
    )(inv2)


def _input_kernel(x_ref, g_ref, w_ref, wf_ref, fb_ref, cos_ref, sin_ref,
                  fq_ref, fk_ref, fv_ref, fg_ref, rq_ref, rk_ref, rv_ref, rg_ref,
                  carry_ref, *, seq_tiles):
    rows = x_ref.shape[0]
    x = x_ref[...]
    ms = jnp.mean(x * x, axis=-1, keepdims=True)
    u = (x * lax.rsqrt(ms + NORM_EPS) * g_ref[...]).astype(jnp.bfloat16)

    def silu(z):
        return z * (1.0 / (1.0 + jnp.exp(-z)))

    def rotary(z):
        cos = cos_ref[...]
        sin = sin_ref[...]
        heads = []
        for h in range(RET_HEADS):
            zh = z[:, h * LANES:(h + 1) * LANES]
            heads.append(zh * cos + pltpu.roll(zh, LANES // 2, axis=1) * sin)
        return jnp.concatenate(heads, axis=1)

    used = 3 * FOX_HEADS
    f = jnp.transpose(_dot(u, wf_ref[...]) + fb_ref[...])[:used, :]
    y = -(jnp.maximum(-f, 0.0) + jnp.log1p(jnp.exp(-jnp.abs(f)))) * LOG2E
    pos = lax.broadcasted_iota(jnp.int32, y.shape, 1)

    def scan_steps(y, shifts):
        for shift in shifts:
            y = y + jnp.where(pos >= shift, pltpu.roll(y, shift, axis=1), 0.0)
        return y

    shifts = [1 << b for b in range(rows.bit_length() - 1)]

    fg_ref[...] = silu(_dot(u, w_ref[3])).astype(jnp.bfloat16)
    y = scan_steps(y, shifts[0:3])
    rg_ref[...] = silu(_dot(u, w_ref[7])).astype(jnp.bfloat16)
    y = scan_steps(y, shifts[3:6])
    rq_ref[...] = rotary(_dot(u, w_ref[4])).astype(jnp.bfloat16)
    y = scan_steps(y, shifts[6:])

    @pl.when(pl.program_id(0) % seq_tiles == 0)
    def _():
        carry_ref[...] = jnp.zeros_like(carry_ref)

    y = y + carry_ref[:, 0:1]
    carry_ref[...] = jnp.broadcast_to(y[:, rows - 1:rows], carry_ref.shape)

    rk_ref[...] = (rotary(_dot(u, w_ref[5])) * (RET_HEAD_DIM ** -0.5)).astype(jnp.bfloat16)

    neg_c = -y
    hi = neg_c.astype(jnp.bfloat16).astype(jnp.float32)
    rest = neg_c - hi
    mid = rest.astype(jnp.bfloat16).astype(jnp.float32)
    lo = rest - mid
    copy = lax.broadcasted_iota(jnp.int32, y.shape, 0) % 3
    piece = jnp.where(copy == 0, hi, jnp.where(copy == 1, mid, lo))
    piece = jnp.transpose(jnp.concatenate([piece, jnp.zeros((LANES - used, rows), jnp.float32)], axis=0))

    fq_ref[...] = (_dot(u, w_ref[0]) * (FOX_HEAD_DIM ** -0.5 * LOG2E)).astype(jnp.bfloat16)

    lane = lax.broadcasted_iota(jnp.int32, piece.shape, 1)
    pair_lanes = 2 * 3
    for p in range(FOX_HEADS // 2):
        shifted = piece if p == 0 else pltpu.roll(piece, LANES - pair_lanes * p, axis=1)
        extra = jnp.where(lane < pair_lanes, shifted, 0.0).astype(jnp.bfloat16)
        fk_ref[:, (2 * p + 1) * LANES:(2 * p + 2) * LANES] = extra

    fk = _dot(u, w_ref[1]).astype(jnp.bfloat16)
    for p in range(FOX_HEADS // 2):
        fk_ref[:, 2 * p * LANES:(2 * p + 1) * LANES] = fk[:, p * LANES:(p + 1) * LANES]
    fv_ref[...] = _dot(u, w_ref[2]).astype(jnp.bfloat16)
    rv_ref[...] = _dot(u, w_ref[6]).astype(jnp.bfloat16)


def _input_stage(x2, gain, w8, wf, fbias, cos_t, sin_t, seq):
    n = x2.shape[0]
    rows = IN_ROWS
    seq_tiles = seq // rows
    act = jax.ShapeDtypeStruct((n, FOX_WIDTH), jnp.bfloat16)
    act_spec = pl.BlockSpec((rows, FOX_WIDTH), lambda i: (i, 0))
    wide = jax.ShapeDtypeStruct((n, 2 * FOX_WIDTH), jnp.bfloat16)
    wide_spec = pl.BlockSpec((rows, 2 * FOX_WIDTH), lambda i: (i, 0))
    tab_spec = pl.BlockSpec((rows, LANES), lambda i: (i % seq_tiles, 0))
    return pl.pallas_call(
        functools.partial(_input_kernel, seq_tiles=seq_tiles),
        grid=(n // rows,),
        in_specs=[
            pl.BlockSpec((rows, D_MODEL), lambda i: (i, 0)),
            pl.BlockSpec((1, D_MODEL), lambda i: (0, 0)),
            pl.BlockSpec((8, D_MODEL, FOX_WIDTH), lambda i: (0, 0, 0)),
            pl.BlockSpec((D_MODEL, LANES), lambda i: (0, 0)),
            pl.BlockSpec((1, LANES), lambda i: (0, 0)),
            tab_spec,
            tab_spec,
        ],
        out_specs=[act_spec, wide_spec] + [act_spec] * 6,
        out_shape=[act, wide] + [act] * 6,
        scratch_shapes=[pltpu.VMEM((3 * FOX_HEADS, LANES), jnp.float32)],
        compiler_params=pltpu.CompilerParams(dimension_semantics=("arbitrary",), vmem_limit_bytes=VMEM_LIMIT),
        name="input_stage",
    )(x2, gain, w8, wf, fbias, cos_t, sin_t)


ONES_ROWS = 16


def _fox_kernel(q_ref, k_ref, v_ref, g_ref, o_ref, vt_ref, s_ref):
    blk = q_ref.shape[1] // 2
    i = pl.program_id(2)
    dh = FOX_HEAD_DIM
    sub = vt_ref.shape[3]
    subs = blk // sub
    half = blk // 2
    slots = s_ref.shape[0]

    @pl.when(i == 0)
    def _():
        ones = jnp.ones((ONES_ROWS, sub), jnp.bfloat16)
        for jb in range(v_ref.shape[1] // sub):
            vt = jnp.transpose(v_ref[0, jb * sub:(jb + 1) * sub, :].astype(jnp.float32))
            for h in range(2):
                vt_ref[h, jb, 0:dh, :] = vt[h * dh:(h + 1) * dh].astype(jnp.bfloat16)
                vt_ref[h, jb, dh:dh + ONES_ROWS, :] = ones

    qts = []
    for qb in range(2):
        qt = jnp.transpose(q_ref[0, qb * blk:(qb + 1) * blk, :].astype(jnp.float32))
        row = lax.broadcasted_iota(jnp.int32, qt.shape, 0)
        per_head = []
        for h in range(2):
            own = jnp.where((row >= h * dh) & (row < (h + 1) * dh), qt, 0.0)
            ones = jnp.where((row >= 3 * h) & (row < 3 * h + 3), 1.0, 0.0)
            per_head.append(jnp.concatenate([own, ones], axis=0).astype(jnp.bfloat16))
        qts.append(per_head)

    def causal(s, t, lo):
        key = lax.broadcasted_iota(jnp.int32, s.shape, 0) + t * sub
        query = lax.broadcasted_iota(jnp.int32, s.shape, 1) + lo
        return jnp.where(key <= query, s, NEG_BIG)

    def produce(unit, slot):
        j, t, h, qb, lo, mask = unit
        start = pl.multiple_of(j * blk + t * sub, sub)
        s = _dot(k_ref[0, pl.ds(start, sub), :], qts[qb][h][:, lo:])
        if mask == "produce":
            s = causal(s, t, lo)
        s_ref[slot, :, lo:] = s
        return jnp.max(s, axis=0, keepdims=True)

    def consume(state, unit, slot, col_max):
        j, t, h, qb, lo, mask = unit
        m, acc = state[qb][h]
        s = s_ref[slot, :, lo:]
        if mask == "consume":
            s = causal(s, t, lo)
            col_max = jnp.max(s, axis=0, keepdims=True)
        m_new = jnp.maximum(m[:, lo:], col_max)
        p = jnp.exp2(s - m_new).astype(jnp.bfloat16)
        acc_new = jnp.exp2(m[:, lo:] - m_new) * acc[:, lo:] + _dot(vt_ref[h, j * subs + t], p)
        if lo:
            m_new = jnp.concatenate([m[:, :lo], m_new], axis=1)
            acc_new = jnp.concatenate([acc[:, :lo], acc_new], axis=1)
        state[qb][h] = (m_new, acc_new)

    def finish(state, qb):
        (_, acc0), (_, acc1) = state[qb]
        out_t = jnp.concatenate([acc0[0:dh] / acc0[dh:dh + 1], acc1[0:dh] / acc1[dh:dh + 1]], axis=0)
        rows = slice(qb * blk, (qb + 1) * blk)
        o_ref[0, rows, :] = (jnp.transpose(out_t) * g_ref[0, rows, :].astype(jnp.float32)).astype(o_ref.dtype)

    def run(state, units, first_max, following=(), after=None):
        pending = dict(enumerate(first_max))
        stream = list(units) + list(following)
        for n, unit in enumerate(units):
            if n + 2 < len(stream):
                pending[n + 2] = produce(stream[n + 2], (n + 2) % slots)
            consume(state, unit, n % slots, pending.pop(n))
            if after and n in after:
                after[n]()
        return tuple(pending[n] for n in sorted(pending))

    def shared(j):
        return [(j, t, h, qb, 0, None) for t in range(subs) for qb in range(2) for h in range(2)]

    def shared_block(j, carry):
        state, first_max = carry
        state = [list(per_block) for per_block in state]
        first_max = run(state, shared(j), first_max, following=shared(j + 1)[:2])
        return tuple(tuple(per_block) for per_block in state), first_max

    init = tuple(
        tuple((jnp.full((1, blk), NEG_BIG, jnp.float32), jnp.zeros((dh + ONES_ROWS, blk), jnp.float32))
              for _ in range(2)) for _ in range(2))
    assert slots == len(shared(0))
    first_max = (produce(shared(0)[0], 0), produce(shared(0)[1], 1))
    state, first_max = lax.fori_loop(0, 2 * i, shared_block, (init, first_max))

    ja, jb = 2 * i, 2 * i + 1
    tail = [
        (ja, 0, 0, 0, 0, "consume"), (ja, 0, 1, 0, 0, "consume"), (ja, 0, 0, 1, 0, None), (ja, 0, 1, 1, 0, None),
        (ja, 1, 0, 0, half, "produce"), (ja, 1, 1, 0, half, "produce"), (ja, 1, 0, 1, 0, None), (ja, 1, 1, 1, 0, None),
        (jb, 0, 0, 1, 0, "produce"), (jb, 0, 1, 1, 0, "produce"),
        (jb, 1, 0, 1, half, "produce"), (jb, 1, 1, 1, half, "produce"),
    ]
    state = [list(per_block) for per_block in state]
    run(state, tail, first_max, after={5: lambda: finish(state, 0)})
    finish(state, 1)


def _fox_stage(fq, fk, fv, fg):
    batch, seq, _ = fq.shape
    blk = FOX_BLOCK
    pairs = FOX_WIDTH // LANES
    tile = pl.BlockSpec((1, 2 * blk, LANES), lambda b, p, i: (b, i, p))
    return pl.pallas_call(
        _fox_kernel,
        grid=(batch, pairs, seq // (2 * blk)),
        in_specs=[
            tile,
            pl.BlockSpec((1, seq, 2 * LANES), lambda b, p, i: (b, 0, p)),
            pl.BlockSpec((1, seq, LANES), lambda b, p, i: (b, 0, p)),
            tile,
        ],
        out_specs=tile,
        out_shape=jax.ShapeDtypeStruct((batch, seq, FOX_WIDTH), jnp.bfloat16),
        scratch_shapes=[
            pltpu.VMEM((2, seq // FOX_KEYS, FOX_HEAD_DIM + ONES_ROWS, FOX_KEYS), jnp.bfloat16),
            pltpu.VMEM((4 * blk // FOX_KEYS, FOX_KEYS, blk), jnp.float32),
        ],
        compiler_params=pltpu.CompilerParams(
            dimension_semantics=("arbitrary", "arbitrary", "arbitrary"), vmem_limit_bytes=VMEM_LIMIT),
        name="fox_stage",
    )(fq, fk, fv, fg)


def _ret_kernel(lg_ref, q_ref, k_ref, v_ref, g_ref, o_ref, state_ref, decay_ref, qdec_ref, kdec_ref, sdec_ref):
    blk = q_ref.shape[1]

    @pl.when((pl.program_id(0) == 0) & (pl.program_id(1) == 0))
    def _():
        row = lax.broadcasted_iota(jnp.int32, (blk, blk), 0)
        col = lax.broadcasted_iota(jnp.int32, (blk, blk), 1)
        dist = jnp.abs(row - col).astype(jnp.float32)
        pos = lax.broadcasted_iota(jnp.int32, (blk, LANES), 0).astype(jnp.float32)
        for h in range(RET_HEADS):
            lg = lg_ref[h]
            decay_ref[h] = jnp.where(col // CHUNK <= row // CHUNK, jnp.exp(lg * dist), 0.0)
            qdec_ref[h] = jnp.exp(lg * (pos + 1.0))
            kdec_ref[h] = jnp.exp(lg * (blk - 1.0 - pos))
            sdec_ref[h] = jnp.exp(lg * jnp.full((8, LANES), blk, jnp.float32))

    @pl.when(pl.program_id(1) == 0)
    def _():
        state_ref[...] = jnp.zeros_like(state_ref)

    for h in range(RET_HEADS):
        cols = slice(h * LANES, (h + 1) * LANES)
        q = q_ref[0, :, cols]
        k = k_ref[0, :, cols]
        v = v_ref[0, :, cols]
        state = state_ref[h]

        scores = _dot_nt(q, k) * decay_ref[h]
        inner = _dot(scores.astype(jnp.bfloat16), v)
        cross = _dot(q, state.astype(jnp.bfloat16)) * qdec_ref[h]
        kd = (k.astype(jnp.float32) * kdec_ref[h]).astype(jnp.bfloat16)
        state_ref[h] = state * sdec_ref[h, 0:1, :] + _dot_tn(kd, v)

        o = inner + cross
        mu = jnp.mean(o, axis=-1, keepdims=True)
        d = o - mu
        var = jnp.mean(d * d, axis=-1, keepdims=True)
        y = d * lax.rsqrt(var + GN_EPS)
        o_ref[0, :, cols] = (y * g_ref[0, :, cols].astype(jnp.float32)).astype(o_ref.dtype)


def _ret_stage(log_gamma, rq, rk, rv, rg):
    batch, seq, _ = rq.shape
    blk = RET_BLOCK
    tile = pl.BlockSpec((1, blk, RET_WIDTH), lambda b, t: (b, t, 0))
    return pl.pallas_call(
        _ret_kernel,
        grid=(batch, seq // blk),
        in_specs=[pl.BlockSpec(memory_space=pltpu.SMEM), tile, tile, tile, tile],
        out_specs=tile,
        out_shape=jax.ShapeDtypeStruct((batch, seq, RET_WIDTH), jnp.bfloat16),
        scratch_shapes=[
            pltpu.VMEM((RET_HEADS, RET_HEAD_DIM, RET_HEAD_DIM), jnp.float32),
            pltpu.VMEM((RET_HEADS, blk, blk), jnp.float32),
            pltpu.VMEM((RET_HEADS, blk, LANES), jnp.float32),
            pltpu.VMEM((RET_HEADS, blk, LANES), jnp.float32),
            pltpu.VMEM((RET_HEADS, 8, LANES), jnp.float32),
        ],
        compiler_params=pltpu.CompilerParams(
            dimension_semantics=("arbitrary", "arbitrary"), vmem_limit_bytes=VMEM_LIMIT),
        name="ret_stage",
    )(log_gamma, rq, rk, rv, rg)


def _output_kernel(mf_ref, mr_ref, wf_ref, wr_ref, g_ref, x_ref, o_ref):
    out = _dot(mf_ref[...], wf_ref[...]) + _dot(mr_ref[...], wr_ref[...])
    ms = jnp.mean(out * out, axis=-1, keepdims=True)
    o_ref[...] = x_ref[...] + out * lax.rsqrt(ms + NORM_EPS) * g_ref[...]


def _output_stage(mf, mr, wo_f, wo_r, gain, x2):
    n = x2.shape[0]
    rows = OUT_ROWS
    return pl.pallas_call(
        _output_kernel,
        grid=(n // rows,),
        in_specs=[
            pl.BlockSpec((rows, FOX_WIDTH), lambda i: (i, 0)),
            pl.BlockSpec((rows, RET_WIDTH), lambda i: (i, 0)),
            pl.BlockSpec((FOX_WIDTH, D_MODEL), lambda i: (0, 0)),
            pl.BlockSpec((RET_WIDTH, D_MODEL), lambda i: (0, 0)),
            pl.BlockSpec((1, D_MODEL), lambda i: (0, 0)),
            pl.BlockSpec((rows, D_MODEL), lambda i: (i, 0)),
        ],
        out_specs=pl.BlockSpec((rows, D_MODEL), lambda i: (i, 0)),
        out_shape=jax.ShapeDtypeStruct((n, D_MODEL), jnp.float32),
        compiler_params=pltpu.CompilerParams(dimension_semantics=("arbitrary",), vmem_limit_bytes=VMEM_LIMIT),
        name="output_stage",
    )(mf, mr, wo_f, wo_r, gain, x2)


def _layer(h, pre_gain, w_in, forget_bias, w_out, post_gain, cos_t, sin_t, log_gamma):
    batch, seq, _ = h.shape
    x2 = h.reshape(batch * seq, D_MODEL)
    wb = w_in.astype(jnp.bfloat16)
    fox_cols = wb[:, :4 * FOX_WIDTH].reshape(D_MODEL, 4, FOX_WIDTH)
    ret_cols = wb[:, 4 * FOX_WIDTH + FOX_HEADS:].reshape(D_MODEL, 4, RET_WIDTH)
    w8 = jnp.concatenate([fox_cols, ret_cols], axis=1).transpose(1, 0, 2)
    wf = jnp.pad(jnp.repeat(wb[:, 4 * FOX_WIDTH:4 * FOX_WIDTH + FOX_HEADS], 3, axis=1),
                 ((0, 0), (0, LANES - 3 * FOX_HEADS)))
    fbias = jnp.pad(jnp.repeat(forget_bias.astype(jnp.float32), 3), (0, LANES - 3 * FOX_HEADS))[None, :]

    fq, fk, fv, fg, rq, rk, rv, rg = _input_stage(
        x2, pre_gain[None, :], w8, wf, fbias, cos_t, sin_t, seq)

    def seq3(a):
        return a.reshape(batch, seq, a.shape[-1])

    mixed_fox = _fox_stage(seq3(fq), seq3(fk), seq3(fv), seq3(fg))
    mixed_ret = _ret_stage(log_gamma, seq3(rq), seq3(rk), seq3(rv), seq3(rg))

    wo = w_out.astype(jnp.bfloat16)
    out = _output_stage(
        mixed_fox.reshape(batch * seq, FOX_WIDTH), mixed_ret.reshape(batch * seq, RET_WIDTH),
        wo[:FOX_WIDTH], wo[FOX_WIDTH:], post_gain[None, :], x2)
    return out.reshape(batch, seq, D_MODEL)


def kernel(x, pre_norm_gain, w_in, fox_forget_bias, w_out, post_norm_gain):
    seq = x.shape[1]
    cos_t, sin_t = _rope_tables(seq)
    log_gamma = jnp.log1p(
        -jnp.exp(jnp.linspace(math.log(1.0 / 32), math.log(1.0 / 512), RET_HEADS))).astype(jnp.float32)
    h = x
    for layer in range(pre_norm_gain.shape[0]):
        h = _layer(h, pre_norm_gain[layer], w_in[layer], fox_forget_bias[layer], w_out[layer],
                   post_norm_gain[layer], cos_t, sin_t, log_gamma)
    return h
```

```python
import functools
import math

import jax
import jax.numpy as jnp
from jax import lax
from jax.experimental import pallas as pl
from jax.experimental.pallas import tpu as pltpu

D_MODEL = 1024
CHUNK = 64
FOX_HEADS = 8
FOX_HEAD_DIM = 64
FOX_WIDTH = FOX_HEADS * FOX_HEAD_DIM
RET_HEADS = 4
RET_HEAD_DIM = 128
RET_WIDTH = RET_HEADS * RET_HEAD_DIM
ROPE_BASE = 10000.0
NORM_EPS = 1e-6
GN_EPS = 1e-5

LANES = 128
LOG2E = math.log2(math.e)
NEG_BIG = -1e30

IN_ROWS = 512
FOX_BLOCK = 512
FOX_KEYS = 256
RET_BLOCK = 256
VMEM_LIMIT = 48 * 1024 * 1024


def _dot(a, b):
    return jnp.dot(a, b, preferred_element_type=jnp.float32)


def _dot_nt(a, b):
    return lax.dot_general(a, b, (((1,), (1,)), ((), ())), preferred_element_type=jnp.float32)


def _dot_tn(a, b):
    return lax.dot_general(a, b, (((0,), (0,)), ((), ())), preferred_element_type=jnp.float32)


def _rope_table_kernel(inv_ref, cos_ref, sin_ref):
    rows = cos_ref.shape[0]
    pos = (lax.broadcasted_iota(jnp.int32, (rows, LANES), 0) + pl.program_id(0) * rows).astype(jnp.float32)
    lane = lax.broadcasted_iota(jnp.int32, (rows, LANES), 1)
    ang = pos * inv_ref[...]
    cos_ref[...] = jnp.cos(ang)
    sin_ref[...] = jnp.where(lane < LANES // 2, -1.0, 1.0) * jnp.sin(ang)


def _rope_tables(seq):
    half = RET_HEAD_DIM // 2
    inv = 1.0 / (ROPE_BASE ** (jnp.arange(half, dtype=jnp.float32) / half))
    inv2 = jnp.concatenate([inv, inv])[None, :]
    rows = 512
    return pl.pallas_call(
        _rope_table_kernel,
        grid=(seq // rows,),
        in_specs=[pl.BlockSpec((1, LANES), lambda i: (0, 0))],
        out_specs=[pl.BlockSpec((rows, LANES), lambda i: (i, 0))] * 2,
        out_shape=[jax.ShapeDtypeStruct((seq, LANES), jnp.float32)] * 2,
        name="rope_tables",
    )(inv2)


def _input_kernel(x_ref, g_ref, w_ref, wf_ref, fb_ref, cos_ref, sin_ref,
                  fq_ref, fk_ref, fv_ref, fg_ref, rq_ref, rk_ref, rv_ref, rg_ref,
                  carry_ref, z_ref, *, seq_tiles):
    rows = x_ref.shape[0]

    @pl.when(pl.program_id(0) % seq_tiles == 0)
    def _():
        carry_ref[...] = jnp.zeros_like(carry_ref)

    x = x_ref[...]
    ms = jnp.mean(x * x, axis=-1, keepdims=True)
    u = (x * lax.rsqrt(ms + NORM_EPS) * g_ref[...]).astype(jnp.bfloat16)

    def silu(z):
        return z * (1.0 / (1.0 + jnp.exp(-z)))

    def rotary(z):
        cos = cos_ref[...]
        sin = sin_ref[...]
        heads = []
        for h in range(RET_HEADS):
            zh = z[:, h * LANES:(h + 1) * LANES]
            heads.append(zh * cos + pltpu.roll(zh, LANES // 2, axis=1) * sin)
        return jnp.concatenate(heads, axis=1)

    def store_fk(z):
        fk = z.astype(jnp.bfloat16)
        for p in range(FOX_HEADS // 2):
            fk_ref[:, 2 * p * LANES:(2 * p + 1) * LANES] = fk[:, p * LANES:(p + 1) * LANES]

    def store(ref, fn):
        def epilogue(z):
            ref[...] = fn(z).astype(jnp.bfloat16)
        return epilogue

    stages = [
        (3, store(fg_ref, silu)),
        (7, store(rg_ref, silu)),
        (4, store(rq_ref, rotary)),
        (5, store(rk_ref, lambda z: rotary(z) * (RET_HEAD_DIM ** -0.5))),
        (0, store(fq_ref, lambda z: z * (FOX_HEAD_DIM ** -0.5 * LOG2E))),
        (1, store_fk),
        (2, store(fv_ref, lambda z: z)),
        (6, store(rv_ref, lambda z: z)),
    ]

    used = 3 * FOX_HEADS
    f = jnp.transpose(_dot(u, wf_ref[...]) + fb_ref[...])[:used, :]
    chain = {"y": -(jnp.maximum(-f, 0.0) + jnp.log1p(jnp.exp(-jnp.abs(f)))) * LOG2E}
    pos = lax.broadcasted_iota(jnp.int32, (used, rows), 1)
    shifts = [1 << b for b in range(rows.bit_length() - 1)]

    def scan_steps(some):
        def piece():
            y = chain["y"]
            for shift in some:
                y = y + jnp.where(pos >= shift, pltpu.roll(y, shift, axis=1), 0.0)
            chain["y"] = y
        return piece

    def add_carry():
        y = chain["y"] + carry_ref[:, 0:1]
        carry_ref[...] = jnp.broadcast_to(y[:, rows - 1:rows], carry_ref.shape)
        chain["y"] = y

    def split_pieces():
        neg_c = -chain["y"]
        hi = neg_c.astype(jnp.bfloat16).astype(jnp.float32)
        rest = neg_c - hi
        mid = rest.astype(jnp.bfloat16).astype(jnp.float32)
        lo = rest - mid
        copy = lax.broadcasted_iota(jnp.int32, neg_c.shape, 0) % 3
        piece = jnp.where(copy == 0, hi, jnp.where(copy == 1, mid, lo))
        padded = jnp.concatenate([piece, jnp.zeros((LANES - used, rows), jnp.float32)], axis=0)
        chain["piece"] = jnp.transpose(padded)

    def store_pieces():
        piece = chain["piece"]
        lane = lax.broadcasted_iota(jnp.int32, piece.shape, 1)
        pair_lanes = 2 * 3
        for p in range(FOX_HEADS // 2):
            shifted = piece if p == 0 else pltpu.roll(piece, LANES - pair_lanes * p, axis=1)
            extra = jnp.where(lane < pair_lanes, shifted, 0.0).astype(jnp.bfloat16)
            fk_ref[:, (2 * p + 1) * LANES:(2 * p + 2) * LANES] = extra

    chain_pieces = [scan_steps(shifts[0:3]), scan_steps(shifts[3:6]), scan_steps(shifts[6:]), add_carry,
                    split_pieces, store_pieces]

    for n, (segment, _) in enumerate(stages):
        z_ref[n % 2] = _dot(u, w_ref[segment])
        if n > 0:
            stages[n - 1][1](z_ref[(n - 1) % 2])
            if n - 1 < len(chain_pieces):
                chain_pieces[n - 1]()
    stages[-1][1](z_ref[(len(stages) - 1) % 2])


def _input_stage(x2, gain, w8, wf, fbias, cos_t, sin_t, seq):
    n = x2.shape[0]
    rows = IN_ROWS
    seq_tiles = seq // rows
    act = jax.ShapeDtypeStruct((n, FOX_WIDTH), jnp.bfloat16)
    act_spec = pl.BlockSpec((rows, FOX_WIDTH), lambda i: (i, 0))
    wide = jax.ShapeDtypeStruct((n, 2 * FOX_WIDTH), jnp.bfloat16)
    wide_spec = pl.BlockSpec((rows, 2 * FOX_WIDTH), lambda i: (i, 0))
    tab_spec = pl.BlockSpec((rows, LANES), lambda i: (i % seq_tiles, 0))
    return pl.pallas_call(
        functools.partial(_input_kernel, seq_tiles=seq_tiles),
        grid=(n // rows,),
        in_specs=[
            pl.BlockSpec((rows, D_MODEL), lambda i: (i, 0)),
            pl.BlockSpec((1, D_MODEL), lambda i: (0, 0)),
            pl.BlockSpec((8, D_MODEL, FOX_WIDTH), lambda i: (0, 0, 0)),
            pl.BlockSpec((D_MODEL, LANES), lambda i: (0, 0)),
            pl.BlockSpec((1, LANES), lambda i: (0, 0)),
            tab_spec,
            tab_spec,
        ],
        out_specs=[act_spec, wide_spec] + [act_spec] * 6,
        out_shape=[act, wide] + [act] * 6,
        scratch_shapes=[
            pltpu.VMEM((3 * FOX_HEADS, LANES), jnp.float32),
            pltpu.VMEM((2, rows, FOX_WIDTH), jnp.float32),
        ],
        compiler_params=pltpu.CompilerParams(dimension_semantics=("arbitrary",), vmem_limit_bytes=VMEM_LIMIT),
        name="input_stage",
    )(x2, gain, w8, wf, fbias, cos_t, sin_t)


ONES_ROWS = 16


def _fox_kernel(q_ref, k_ref, v_ref, g_ref, o_ref, vt_ref, s_ref):
    blk = q_ref.shape[1] // 2
    i = pl.program_id(2)
    dh = FOX_HEAD_DIM
    sub = vt_ref.shape[3]
    subs = blk // sub
    half = blk // 2
    slots = s_ref.shape[0]

    @pl.when(i == 0)
    def _():
        ones = jnp.ones((ONES_ROWS, sub), jnp.bfloat16)
        for jb in range(v_ref.shape[1] // sub):
            vt = jnp.transpose(v_ref[0, jb * sub:(jb + 1) * sub, :].astype(jnp.float32))
            for h in range(2):
                vt_ref[h, jb, 0:dh, :] = vt[h * dh:(h + 1) * dh].astype(jnp.bfloat16)
                vt_ref[h, jb, dh:dh + ONES_ROWS, :] = ones

    def query_operands(qb):
        qt = jnp.transpose(q_ref[0, qb * blk:(qb + 1) * blk, :].astype(jnp.float32))
        row = lax.broadcasted_iota(jnp.int32, qt.shape, 0)
        per_head = []
        for h in range(2):
            own = jnp.where((row >= h * dh) & (row < (h + 1) * dh), qt, 0.0)
            ones = jnp.where((row >= 3 * h) & (row < 3 * h + 3), 1.0, 0.0)
            per_head.append(jnp.concatenate([own, ones], axis=0).astype(jnp.bfloat16))
        return per_head

    qts = [query_operands(0)]

    def causal(s, t, lo):
        key = lax.broadcasted_iota(jnp.int32, s.shape, 0) + t * sub
        query = lax.broadcasted_iota(jnp.int32, s.shape, 1) + lo
        return jnp.where(key <= query, s, NEG_BIG)

    def produce(unit, slot):
        j, t, h, qb, lo, mask = unit
        start = pl.multiple_of(j * blk + t * sub, sub)
        s = _dot(k_ref[0, pl.ds(start, sub), :], qts[qb][h][:, lo:])
        if mask == "produce":
            s = causal(s, t, lo)
        s_ref[slot, :, lo:] = s
        return jnp.max(s, axis=0, keepdims=True)

    def consume(state, unit, slot, col_max):
        j, t, h, qb, lo, mask = unit
        m, acc = state[qb][h]
        s = s_ref[slot, :, lo:]
        if mask == "consume":
            s = causal(s, t, lo)
            col_max = jnp.max(s, axis=0, keepdims=True)
        m_new = jnp.maximum(m[:, lo:], col_max)
        p = jnp.exp2(s - m_new).astype(jnp.bfloat16)
        acc_new = jnp.exp2(m[:, lo:] - m_new) * acc[:, lo:] + _dot(vt_ref[h, j * subs + t], p)
        if lo:
            m_new = jnp.concatenate([m[:, :lo], m_new], axis=1)
            acc_new = jnp.concatenate([acc[:, :lo], acc_new], axis=1)
        state[qb][h] = (m_new, acc_new)

    def finish(state, qb):
        (_, acc0), (_, acc1) = state[qb]
        out_t = jnp.concatenate([acc0[0:dh] / acc0[dh:dh + 1], acc1[0:dh] / acc1[dh:dh + 1]], axis=0)
        rows = slice(qb * blk, (qb + 1) * blk)
        o_ref[0, rows, :] = (jnp.transpose(out_t) * g_ref[0, rows, :].astype(jnp.float32)).astype(o_ref.dtype)

    def run(state, units, first_max, following=(), after=None):
        pending = dict(enumerate(first_max))
        stream = list(units) + list(following)
        for n, unit in enumerate(units):
            if n + 2 < len(stream):
                pending[n + 2] = produce(stream[n + 2], (n + 2) % slots)
            consume(state, unit, n % slots, pending.pop(n))
            if after and n in after:
                after[n]()
        return tuple(pending[n] for n in sorted(pending))

    def shared(j):
        return [(j, t, h, qb, 0, None) for t in range(subs) for qb in range(2) for h in range(2)]

    def shared_block(j, carry):
        state, first_max = carry
        state = [list(per_block) for per_block in state]
        first_max = run(state, shared(j), first_max, following=shared(j + 1)[:2])
        return tuple(tuple(per_block) for per_block in state), first_max

    init = tuple(
        tuple((jnp.full((1, blk), NEG_BIG, jnp.float32), jnp.zeros((dh + ONES_ROWS, blk), jnp.float32))
              for _ in range(2)) for _ in range(2))
    assert slots == len(shared(0))
    first_max = (produce(shared(0)[0], 0), produce(shared(0)[1], 1))
    qts.append(query_operands(1))
    state, first_max = lax.fori_loop(0, 2 * i, shared_block, (init, first_max))

    ja, jb = 2 * i, 2 * i + 1
    tail = [
        (ja, 0, 0, 0, 0, "consume"), (ja, 0, 1, 0, 0, "consume"), (ja, 0, 0, 1, 0, None), (ja, 0, 1, 1, 0, None),
        (ja, 1, 0, 0, half, "produce"), (ja, 1, 1, 0, half, "produce"), (ja, 1, 0, 1, 0, None), (ja, 1, 1, 1, 0, None),
        (jb, 0, 0, 1, 0, "produce"), (jb, 0, 1, 1, 0, "produce"),
        (jb, 1, 0, 1, half, "produce"), (jb, 1, 1, 1, half, "produce"),
    ]
    state = [list(per_block) for per_block in state]
    run(state, tail, first_max, after={5: lambda: finish(state, 0)})
    finish(state, 1)


def _fox_stage(fq, fk, fv, fg):
    batch, seq, _ = fq.shape
    blk = FOX_BLOCK
    pairs = FOX_WIDTH // LANES
    tile = pl.BlockSpec((1, 2 * blk, LANES), lambda b, p, i: (b, i, p))
    return pl.pallas_call(
        _fox_kernel,
        grid=(batch, pairs, seq // (2 * blk)),
        in_specs=[
            tile,
            pl.BlockSpec((1, seq, 2 * LANES), lambda b, p, i: (b, 0, p)),
            pl.BlockSpec((1, seq, LANES), lambda b, p, i: (b, 0, p)),
            tile,
        ],
        out_specs=tile,
        out_shape=jax.ShapeDtypeStruct((batch, seq, FOX_WIDTH), jnp.bfloat16),
        scratch_shapes=[
            pltpu.VMEM((2, seq // FOX_KEYS, FOX_HEAD_DIM + ONES_ROWS, FOX_KEYS), jnp.bfloat16),
            pltpu.VMEM((4 * blk // FOX_KEYS, FOX_KEYS, blk), jnp.float32),
        ],
        compiler_params=pltpu.CompilerParams(
            dimension_semantics=("arbitrary", "arbitrary", "arbitrary"), vmem_limit_bytes=VMEM_LIMIT),
        name="fox_stage",
    )(fq, fk, fv, fg)


def _ret_kernel(lg_ref, q_ref, k_ref, v_ref, g_ref, mf_ref, wo_ref, gain_ref, x_ref, o_ref,
                mixed_ref, state_ref, decay_ref, qdec_ref, kdec_ref, sdec_ref):
    blk = q_ref.shape[1]

    @pl.when((pl.program_id(0) == 0) & (pl.program_id(1) == 0))
    def _():
        row = lax.broadcasted_iota(jnp.int32, (blk, blk), 0)
        col = lax.broadcasted_iota(jnp.int32, (blk, blk), 1)
        dist = jnp.abs(row - col).astype(jnp.float32)
        pos = lax.broadcasted_iota(jnp.int32, (blk, LANES), 0).astype(jnp.float32)
        for h in range(RET_HEADS):
            lg = lg_ref[h]
            decay_ref[h] = jnp.where(col // CHUNK <= row // CHUNK, jnp.exp(lg * dist), 0.0)
            qdec_ref[h] = jnp.exp(lg * (pos + 1.0))
            kdec_ref[h] = jnp.exp(lg * (blk - 1.0 - pos))
            sdec_ref[h] = jnp.exp(lg * jnp.full((8, LANES), blk, jnp.float32))

    t = pl.program_id(1)

    @pl.when(t == 0)
    def _():
        state_ref[...] = jnp.zeros_like(state_ref)
        mixed_ref[...] = jnp.zeros_like(mixed_ref)

    previous = jnp.concatenate([mf_ref[0], mixed_ref[(t + 1) % 2]], axis=1)
    projected = []
    for h in range(RET_HEADS):
        cols = slice(h * LANES, (h + 1) * LANES)
        q = q_ref[0, :, cols]
        k = k_ref[0, :, cols]
        v = v_ref[0, :, cols]
        state = state_ref[h]

        scores = _dot_nt(q, k)
        cross = _dot(q, state.astype(jnp.bfloat16))
        if h % 2 == 0:
            half_cols = slice(h // 2 * (D_MODEL // 2), (h // 2 + 1) * (D_MODEL // 2))
            projected.append(_dot(previous, wo_ref[:, half_cols]))
        inner = _dot((scores * decay_ref[h]).astype(jnp.bfloat16), v)
        kd = (k.astype(jnp.float32) * kdec_ref[h]).astype(jnp.bfloat16)
        state_ref[h] = state * sdec_ref[h, 0:1, :] + _dot_tn(kd, v)

        o = inner + cross * qdec_ref[h]
        mu = jnp.mean(o, axis=-1, keepdims=True)
        d = o - mu
        var = jnp.mean(d * d, axis=-1, keepdims=True)
        y = d * lax.rsqrt(var + GN_EPS)
        mixed_ref[t % 2, :, cols] = (y * g_ref[0, :, cols].astype(jnp.float32)).astype(mixed_ref.dtype)

    out = jnp.concatenate(projected, axis=1)
    ms = jnp.mean(out * out, axis=-1, keepdims=True)
    o_ref[0] = x_ref[0] + out * lax.rsqrt(ms + NORM_EPS) * gain_ref[...]


def _ret_output_stage(log_gamma, rq, rk, rv, rg, mixed_fox, wo, gain, x):
    batch, seq, _ = rq.shape
    blk = RET_BLOCK
    steps = seq // blk
    tile = pl.BlockSpec((1, blk, RET_WIDTH), lambda b, t: (b, jnp.minimum(t, steps - 1), 0))
    lagged = pl.BlockSpec((1, blk, FOX_WIDTH), lambda b, t: (b, jnp.maximum(t - 1, 0), 0))
    wide = pl.BlockSpec((1, blk, D_MODEL), lambda b, t: (b, jnp.maximum(t - 1, 0), 0))
    return pl.pallas_call(
        _ret_kernel,
        grid=(batch, steps + 1),
        in_specs=[
            pl.BlockSpec(memory_space=pltpu.SMEM), tile, tile, tile, tile, lagged,
            pl.BlockSpec((FOX_WIDTH + RET_WIDTH, D_MODEL), lambda b, t: (0, 0)),
            pl.BlockSpec((1, D_MODEL), lambda b, t: (0, 0)),
            wide,
        ],
        out_specs=wide,
        out_shape=jax.ShapeDtypeStruct((batch, seq, D_MODEL), jnp.float32),
        scratch_shapes=[
            pltpu.VMEM((2, blk, RET_WIDTH), jnp.bfloat16),
            pltpu.VMEM((RET_HEADS, RET_HEAD_DIM, RET_HEAD_DIM), jnp.float32),
            pltpu.VMEM((RET_HEADS, blk, blk), jnp.float32),
            pltpu.VMEM((RET_HEADS, blk, LANES), jnp.float32),
            pltpu.VMEM((RET_HEADS, blk, LANES), jnp.float32),
            pltpu.VMEM((RET_HEADS, 8, LANES), jnp.float32),
        ],
        compiler_params=pltpu.CompilerParams(
            dimension_semantics=("arbitrary", "arbitrary"), vmem_limit_bytes=VMEM_LIMIT),
        name="ret_output_stage",
    )(log_gamma, rq, rk, rv, rg, mixed_fox, wo, gain, x)


def _layer(h, pre_gain, w_in, forget_bias, w_out, post_gain, cos_t, sin_t, log_gamma):
    batch, seq, _ = h.shape
    x2 = h.reshape(batch * seq, D_MODEL)
    wb = w_in.astype(jnp.bfloat16)
    fox_cols = wb[:, :4 * FOX_WIDTH].reshape(D_MODEL, 4, FOX_WIDTH)
    ret_cols = wb[:, 4 * FOX_WIDTH + FOX_HEADS:].reshape(D_MODEL, 4, RET_WIDTH)
    w8 = jnp.concatenate([fox_cols, ret_cols], axis=1).transpose(1, 0, 2)
    wf = jnp.pad(jnp.repeat(wb[:, 4 * FOX_WIDTH:4 * FOX_WIDTH + FOX_HEADS], 3, axis=1),
                 ((0, 0), (0, LANES - 3 * FOX_HEADS)))
    fbias = jnp.pad(jnp.repeat(forget_bias.astype(jnp.float32), 3), (0, LANES - 3 * FOX_HEADS))[None, :]

    fq, fk, fv, fg, rq, rk, rv, rg = _input_stage(
        x2, pre_gain[None, :], w8, wf, fbias, cos_t, sin_t, seq)

    def seq3(a):
        return a.reshape(batch, seq, a.shape[-1])

    mixed_fox = _fox_stage(seq3(fq), seq3(fk), seq3(fv), seq3(fg))
    return _ret_output_stage(log_gamma, seq3(rq), seq3(rk), seq3(rv), seq3(rg), mixed_fox,
                             w_out.astype(jnp.bfloat16), post_gain[None, :], h)


def kernel(x, pre_norm_gain, w_in, fox_forget_bias, w_out, post_norm_gain):
    seq = x.shape[1]
    cos_t, sin_t = _rope_tables(seq)
    log_gamma = jnp.log1p(
        -jnp.exp(jnp.linspace(math.log(1.0 / 32), math.log(1.0 / 512), RET_HEADS))).astype(jnp.float32)
    h = x
    for layer in range(pre_norm_gain.shape[0]):
        h = _layer(h, pre_norm_gain[layer], w_in[layer], fox_forget_bias[layer], w_out[layer],
                   post_norm_gain[layer], cos_t, sin_t, log_gamma)
    return h
```

```python
import functools
import math

import jax
import jax.numpy as jnp
from jax import lax
from jax.experimental import pallas as pl
from jax.experimental.pallas import tpu as pltpu

D_MODEL = 1024
CHUNK = 64
FOX_HEADS = 8
FOX_HEAD_DIM = 64
FOX_WIDTH = FOX_HEADS * FOX_HEAD_DIM
RET_HEADS = 4
RET_HEAD_DIM = 128
RET_WIDTH = RET_HEADS * RET_HEAD_DIM
ROPE_BASE = 10000.0
NORM_EPS = 1e-6
GN_EPS = 1e-5

LANES = 128
LOG2E = math.log2(math.e)
NEG_BIG = -1e30

IN_ROWS = 512
FOX_BLOCK = 512
FOX_KEYS = 256
RET_BLOCK = 256
VMEM_LIMIT = 48 * 1024 * 1024


def _dot(a, b):
    return jnp.dot(a, b, preferred_element_type=jnp.float32)


def _dot_nt(a, b):
    return lax.dot_general(a, b, (((1,), (1,)), ((), ())), preferred_element_type=jnp.float32)


def _dot_tn(a, b):
    return lax.dot_general(a, b, (((0,), (0,)), ((), ())), preferred_element_type=jnp.float32)


def _rope_table_kernel(inv_ref, cos_ref, sin_ref):
    rows = cos_ref.shape[0]
    pos = (lax.broadcasted_iota(jnp.int32, (rows, LANES), 0) + pl.program_id(0) * rows).astype(jnp.float32)
    lane = lax.broadcasted_iota(jnp.int32, (rows, LANES), 1)
    ang = pos * inv_ref[...]
    cos_ref[...] = jnp.cos(ang)
    sin_ref[...] = jnp.where(lane < LANES // 2, -1.0, 1.0) * jnp.sin(ang)


def _rope_tables(seq):
    half = RET_HEAD_DIM // 2
    inv = 1.0 / (ROPE_BASE ** (jnp.arange(half, dtype=jnp.float32) / half))
    inv2 = jnp.concatenate([inv, inv])[None, :]
    rows = 512
    return pl.pallas_call(
        _rope_table_kernel,
        grid=(seq // rows,),
        in_specs=[pl.BlockSpec((1, LANES), lambda i: (0, 0))],
        out_specs=[pl.BlockSpec((rows, LANES), lambda i: (i, 0))] * 2,
        out_shape=[jax.ShapeDtypeStruct((seq, LANES), jnp.float32)] * 2,
        name="rope_tables",
    )(inv2)


def _input_kernel(x_ref, g_ref, w_ref, wf_ref, fb_ref, cos_ref, sin_ref,
                  fq_ref, fk_ref, fv_ref, fg_ref, rq_ref, rk_ref, rv_ref, rg_ref,
                  carry_ref, z_ref, *, seq_tiles):
    rows = x_ref.shape[0]

    @pl.when(pl.program_id(0) % seq_tiles == 0)
    def _():
        carry_ref[...] = jnp.zeros_like(carry_ref)

    x = x_ref[...]
    ms = jnp.mean(x * x, axis=-1, keepdims=True)
    u = (x * lax.rsqrt(ms + NORM_EPS) * g_ref[...]).astype(jnp.bfloat16)

    def silu(z):
        return z * (1.0 / (1.0 + jnp.exp(-z)))

    def rotary(z):
        cos = cos_ref[...]
        sin = sin_ref[...]
        heads = []
        for h in range(RET_HEADS):
            zh = z[:, h * LANES:(h + 1) * LANES]
            heads.append(zh * cos + pltpu.roll(zh, LANES // 2, axis=1) * sin)
        return jnp.concatenate(heads, axis=1)

    def store_fk(z):
        fk = z.astype(jnp.bfloat16)
        for p in range(FOX_HEADS // 2):
            fk_ref[:, 2 * p * LANES:(2 * p + 1) * LANES] = fk[:, p * LANES:(p + 1) * LANES]

    def store(ref, fn):
        def epilogue(z):
            ref[...] = fn(z).astype(jnp.bfloat16)
        return epilogue

    stages = [
        (3, store(fg_ref, silu)),
        (7, store(rg_ref, silu)),
        (4, store(rq_ref, rotary)),
        (5, store(rk_ref, lambda z: rotary(z) * (RET_HEAD_DIM ** -0.5))),
        (0, store(fq_ref, lambda z: jnp.transpose(z * (FOX_HEAD_DIM ** -0.5 * LOG2E)))),
        (1, store_fk),
        (2, store(fv_ref, lambda z: z)),
        (6, store(rv_ref, lambda z: z)),
    ]

    used = 3 * FOX_HEADS
    f = jnp.transpose(_dot(u, wf_ref[...]) + fb_ref[...])[:used, :]
    chain = {"y": -(jnp.maximum(-f, 0.0) + jnp.log1p(jnp.exp(-jnp.abs(f)))) * LOG2E}
    pos = lax.broadcasted_iota(jnp.int32, (used, rows), 1)
    shifts = [1 << b for b in range(rows.bit_length() - 1)]

    def scan_steps(some):
        def piece():
            y = chain["y"]
            for shift in some:
                y = y + jnp.where(pos >= shift, pltpu.roll(y, shift, axis=1), 0.0)
            chain["y"] = y
        return piece

    def add_carry():
        y = chain["y"] + carry_ref[:, 0:1]
        carry_ref[...] = jnp.broadcast_to(y[:, rows - 1:rows], carry_ref.shape)
        chain["y"] = y

    def split_pieces():
        neg_c = -chain["y"]
        hi = neg_c.astype(jnp.bfloat16).astype(jnp.float32)
        rest = neg_c - hi
        mid = rest.astype(jnp.bfloat16).astype(jnp.float32)
        lo = rest - mid
        copy = lax.broadcasted_iota(jnp.int32, neg_c.shape, 0) % 3
        piece = jnp.where(copy == 0, hi, jnp.where(copy == 1, mid, lo))
        padded = jnp.concatenate([piece, jnp.zeros((LANES - used, rows), jnp.float32)], axis=0)
        chain["piece"] = jnp.transpose(padded)

    def store_pieces():
        piece = chain["piece"]
        lane = lax.broadcasted_iota(jnp.int32, piece.shape, 1)
        pair_lanes = 2 * 3
        for p in range(FOX_HEADS // 2):
            shifted = piece if p == 0 else pltpu.roll(piece, LANES - pair_lanes * p, axis=1)
            extra = jnp.where(lane < pair_lanes, shifted, 0.0).astype(jnp.bfloat16)
            fk_ref[:, (2 * p + 1) * LANES:(2 * p + 2) * LANES] = extra

    chain_pieces = [scan_steps(shifts[0:3]), scan_steps(shifts[3:6]), scan_steps(shifts[6:]), add_carry,
                    split_pieces, store_pieces]

    for n, (segment, _) in enumerate(stages):
        z_ref[n % 2] = _dot(u, w_ref[segment])
        if n > 0:
            stages[n - 1][1](z_ref[(n - 1) % 2])
            if n - 1 < len(chain_pieces):
                chain_pieces[n - 1]()
    stages[-1][1](z_ref[(len(stages) - 1) % 2])


def _input_stage(x2, gain, w8, wf, fbias, cos_t, sin_t, seq):
    n = x2.shape[0]
    rows = IN_ROWS
    seq_tiles = seq // rows
    act = jax.ShapeDtypeStruct((n, FOX_WIDTH), jnp.bfloat16)
    act_spec = pl.BlockSpec((rows, FOX_WIDTH), lambda i: (i, 0))
    wide = jax.ShapeDtypeStruct((n, 2 * FOX_WIDTH), jnp.bfloat16)
    wide_spec = pl.BlockSpec((rows, 2 * FOX_WIDTH), lambda i: (i, 0))
    tab_spec = pl.BlockSpec((rows, LANES), lambda i: (i % seq_tiles, 0))
    return pl.pallas_call(
        functools.partial(_input_kernel, seq_tiles=seq_tiles),
        grid=(n // rows,),
        in_specs=[
            pl.BlockSpec((rows, D_MODEL), lambda i: (i, 0)),
            pl.BlockSpec((1, D_MODEL), lambda i: (0, 0)),
            pl.BlockSpec((8, D_MODEL, FOX_WIDTH), lambda i: (0, 0, 0)),
            pl.BlockSpec((D_MODEL, LANES), lambda i: (0, 0)),
            pl.BlockSpec((1, LANES), lambda i: (0, 0)),
            tab_spec,
            tab_spec,
        ],
        out_specs=[pl.BlockSpec((FOX_WIDTH, rows), lambda i: (0, i)), wide_spec] + [act_spec] * 6,
        out_shape=[jax.ShapeDtypeStruct((FOX_WIDTH, n), jnp.bfloat16), wide] + [act] * 6,
        scratch_shapes=[
            pltpu.VMEM((3 * FOX_HEADS, LANES), jnp.float32),
            pltpu.VMEM((2, rows, FOX_WIDTH), jnp.float32),
        ],
        compiler_params=pltpu.CompilerParams(dimension_semantics=("arbitrary",), vmem_limit_bytes=VMEM_LIMIT),
        name="input_stage",
    )(x2, gain, w8, wf, fbias, cos_t, sin_t)


ONES_ROWS = 16


def _fox_kernel(q_ref, k_ref, v_ref, g_ref, o_ref, vt_ref, s_ref, bias_ref):
    blk = g_ref.shape[1] // 2
    i = pl.program_id(2)
    dh = FOX_HEAD_DIM
    sub = vt_ref.shape[3]
    subs = blk // sub
    half = blk // 2
    slots = s_ref.shape[0]

    @pl.when(i == 0)
    def _():
        key = lax.broadcasted_iota(jnp.int32, bias_ref.shape, 0)
        query = lax.broadcasted_iota(jnp.int32, bias_ref.shape, 1)
        bias_ref[...] = jnp.where(key <= query, 0.0, NEG_BIG)
        ones = jnp.ones((ONES_ROWS, sub), jnp.bfloat16)
        for jb in range(v_ref.shape[1] // sub):
            vt = jnp.transpose(v_ref[0, jb * sub:(jb + 1) * sub, :].astype(jnp.float32))
            for h in range(2):
                vt_ref[h, jb, 0:dh, :] = vt[h * dh:(h + 1) * dh].astype(jnp.bfloat16)
                vt_ref[h, jb, dh:dh + ONES_ROWS, :] = ones

    def query_operands(qb):
        qt = q_ref[:, qb * blk:(qb + 1) * blk].astype(jnp.float32)
        row = lax.broadcasted_iota(jnp.int32, qt.shape, 0)
        per_head = []
        for h in range(2):
            own = jnp.where((row >= h * dh) & (row < (h + 1) * dh), qt, 0.0)
            ones = jnp.where((row >= 3 * h) & (row < 3 * h + 3), 1.0, 0.0)
            per_head.append(jnp.concatenate([own, ones], axis=0).astype(jnp.bfloat16))
        return per_head

    qts = [query_operands(0), query_operands(1)]

    def causal(s, t, lo):
        assert t * sub == lo
        return s + bias_ref[:, 0:s.shape[1]]

    def produce(unit, slot):
        j, t, h, qb, lo, mask = unit
        start = pl.multiple_of(j * blk + t * sub, sub)
        s = _dot(k_ref[0, pl.ds(start, sub), :], qts[qb][h][:, lo:])
        if mask == "produce":
            s = causal(s, t, lo)
        s_ref[slot, :, lo:] = s
        return jnp.max(s, axis=0, keepdims=True)

    def consume(state, unit, slot, col_max):
        j, t, h, qb, lo, mask = unit
        m, acc = state[qb][h]
        s = s_ref[slot, :, lo:]
        if mask == "consume":
            s = causal(s, t, lo)
            col_max = jnp.max(s, axis=0, keepdims=True)
        m_new = jnp.maximum(m[:, lo:], col_max)
        p = jnp.exp2(s - m_new).astype(jnp.bfloat16)
        acc_new = jnp.exp2(m[:, lo:] - m_new) * acc[:, lo:] + _dot(vt_ref[h, j * subs + t], p)
        if lo:
            m_new = jnp.concatenate([m[:, :lo], m_new], axis=1)
            acc_new = jnp.concatenate([acc[:, :lo], acc_new], axis=1)
        state[qb][h] = (m_new, acc_new)

    def finish(state, qb):
        (_, acc0), (_, acc1) = state[qb]
        out_t = jnp.concatenate([acc0[0:dh] / acc0[dh:dh + 1], acc1[0:dh] / acc1[dh:dh + 1]], axis=0)
        rows = slice(qb * blk, (qb + 1) * blk)
        o_ref[0, rows, :] = (jnp.transpose(out_t) * g_ref[0, rows, :].astype(jnp.float32)).astype(o_ref.dtype)

    def run(state, units, first_max, following=(), after=None):
        pending = dict(enumerate(first_max))
        stream = list(units) + list(following)
        for n, unit in enumerate(units):
            if n + 2 < len(stream):
                pending[n + 2] = produce(stream[n + 2], (n + 2) % slots)
            consume(state, unit, n % slots, pending.pop(n))
            if after and n in after:
                after[n]()
        return tuple(pending[n] for n in sorted(pending))

    def shared(j):
        return [(j, t, h, qb, 0, None) for t in range(subs) for qb in range(2) for h in range(2)]

    def shared_blocks(jj, carry):
        state, first_max = carry
        state = [list(per_block) for per_block in state]
        first_max = run(state, shared(2 * jj) + shared(2 * jj + 1), first_max, following=shared(2 * jj + 2)[:2])
        return tuple(tuple(per_block) for per_block in state), first_max

    init = tuple(
        tuple((jnp.full((1, blk), NEG_BIG, jnp.float32), jnp.zeros((dh + ONES_ROWS, blk), jnp.float32))
              for _ in range(2)) for _ in range(2))
    assert slots == len(shared(0))
    first_max = (produce(shared(0)[0], 0), produce(shared(0)[1], 1))
    state, first_max = lax.fori_loop(0, i, shared_blocks, (init, first_max))

    ja, jb = 2 * i, 2 * i + 1
    tail = [
        (ja, 0, 0, 0, 0, "consume"), (ja, 0, 1, 0, 0, "consume"), (ja, 0, 0, 1, 0, None), (ja, 0, 1, 1, 0, None),
        (ja, 1, 0, 0, half, "produce"), (ja, 1, 1, 0, half, "produce"), (ja, 1, 0, 1, 0, None), (ja, 1, 1, 1, 0, None),
        (jb, 0, 0, 1, 0, "produce"), (jb, 0, 1, 1, 0, "produce"),
        (jb, 1, 0, 1, half, "produce"), (jb, 1, 1, 1, half, "produce"),
    ]
    state = [list(per_block) for per_block in state]
    run(state, tail, first_max, after={5: lambda: finish(state, 0)})
    finish(state, 1)


def _fox_stage(fq_t, fk, fv, fg):
    batch, seq, _ = fk.shape
    blk = FOX_BLOCK
    pairs = FOX_WIDTH // LANES
    steps = seq // (2 * blk)
    tile = pl.BlockSpec((1, 2 * blk, LANES), lambda b, p, i: (b, i, p))
    return pl.pallas_call(
        _fox_kernel,
        grid=(batch, pairs, steps),
        in_specs=[
            pl.BlockSpec((LANES, 2 * blk), lambda b, p, i: (p, b * steps + i)),
            pl.BlockSpec((1, seq, 2 * LANES), lambda b, p, i: (b, 0, p)),
            pl.BlockSpec((1, seq, LANES), lambda b, p, i: (b, 0, p)),
            tile,
        ],
        out_specs=tile,
        out_shape=jax.ShapeDtypeStruct((batch, seq, FOX_WIDTH), jnp.bfloat16),
        scratch_shapes=[
            pltpu.VMEM((2, seq // FOX_KEYS, FOX_HEAD_DIM + ONES_ROWS, FOX_KEYS), jnp.bfloat16),
            pltpu.VMEM((4 * blk // FOX_KEYS, FOX_KEYS, blk), jnp.float32),
            pltpu.VMEM((FOX_KEYS, blk), jnp.float32),
        ],
        compiler_params=pltpu.CompilerParams(
            dimension_semantics=("arbitrary", "arbitrary", "arbitrary"), vmem_limit_bytes=VMEM_LIMIT),
        name="fox_stage",
    )(fq_t, fk, fv, fg)


def _ret_kernel(lg_ref, q_ref, k_ref, v_ref, g_ref, mf_ref, wo_ref, gain_ref, x_ref, o_ref,
                mixed_ref, state_ref, decay_ref, qdec_ref, kdec_ref, sdec_ref):
    blk = q_ref.shape[1]

    @pl.when((pl.program_id(0) == 0) & (pl.program_id(1) == 0))
    def _():
        row = lax.broadcasted_iota(jnp.int32, (blk, blk), 0)
        col = lax.broadcasted_iota(jnp.int32, (blk, blk), 1)
        dist = jnp.abs(row - col).astype(jnp.float32)
        pos = lax.broadcasted_iota(jnp.int32, (blk, LANES), 0).astype(jnp.float32)
        for h in range(RET_HEADS):
            lg = lg_ref[h]
            decay_ref[h] = jnp.where(col // CHUNK <= row // CHUNK, jnp.exp(lg * dist), 0.0)
            qdec_ref[h] = jnp.exp(lg * (pos + 1.0))
            kdec_ref[h] = jnp.exp(lg * (blk - 1.0 - pos))
            sdec_ref[h] = jnp.exp(lg * jnp.full((8, LANES), blk, jnp.float32))

    t = pl.program_id(1)

    @pl.when(t == 0)
    def _():
        state_ref[...] = jnp.zeros_like(state_ref)
        mixed_ref[...] = jnp.zeros_like(mixed_ref)

    previous = jnp.concatenate([mf_ref[0], mixed_ref[(t + 1) % 2]], axis=1)
    projected = []
    for h in range(RET_HEADS):
        cols = slice(h * LANES, (h + 1) * LANES)
        q = q_ref[0, :, cols]
        k = k_ref[0, :, cols]
        v = v_ref[0, :, cols]
        state = state_ref[h]

        scores = _dot_nt(q, k)
        cross = _dot(q, state.astype(jnp.bfloat16))
        if h % 2 == 0:
            half_cols = slice(h // 2 * (D_MODEL // 2), (h // 2 + 1) * (D_MODEL // 2))
            projected.append(_dot(previous, wo_ref[:, half_cols]))
        inner = _dot((scores * decay_ref[h]).astype(jnp.bfloat16), v)
        kd = (k.astype(jnp.float32) * kdec_ref[h]).astype(jnp.bfloat16)
        state_ref[h] = state * sdec_ref[h, 0:1, :] + _dot_tn(kd, v)

        o = inner + cross * qdec_ref[h]
        mu = jnp.mean(o, axis=-1, keepdims=True)
        d = o - mu
        var = jnp.mean(d * d, axis=-1, keepdims=True)
        y = d * lax.rsqrt(var + GN_EPS)
        mixed_ref[t % 2, :, cols] = (y * g_ref[0, :, cols].astype(jnp.float32)).astype(mixed_ref.dtype)

    out = jnp.concatenate(projected, axis=1)
    ms = jnp.mean(out * out, axis=-1, keepdims=True)
    o_ref[0] = x_ref[0] + out * lax.rsqrt(ms + NORM_EPS) * gain_ref[...]


def _ret_output_stage(log_gamma, rq, rk, rv, rg, mixed_fox, wo, gain, x):
    batch, seq, _ = rq.shape
    blk = RET_BLOCK
    steps = seq // blk
    tile = pl.BlockSpec((1, blk, RET_WIDTH), lambda b, t: (b, jnp.minimum(t, steps - 1), 0))
    lagged = pl.BlockSpec((1, blk, FOX_WIDTH), lambda b, t: (b, jnp.maximum(t - 1, 0), 0))
    wide = pl.BlockSpec((1, blk, D_MODEL), lambda b, t: (b, jnp.maximum(t - 1, 0), 0))
    return pl.pallas_call(
        _ret_kernel,
        grid=(batch, steps + 1),
        in_specs=[
            pl.BlockSpec(memory_space=pltpu.SMEM), tile, tile, tile, tile, lagged,
            pl.BlockSpec((FOX_WIDTH + RET_WIDTH, D_MODEL), lambda b, t: (0, 0)),
            pl.BlockSpec((1, D_MODEL), lambda b, t: (0, 0)),
            wide,
        ],
        out_specs=wide,
        out_shape=jax.ShapeDtypeStruct((batch, seq, D_MODEL), jnp.float32),
        scratch_shapes=[
            pltpu.VMEM((2, blk, RET_WIDTH), jnp.bfloat16),
            pltpu.VMEM((RET_HEADS, RET_HEAD_DIM, RET_HEAD_DIM), jnp.float32),
            pltpu.VMEM((RET_HEADS, blk, blk), jnp.float32),
            pltpu.VMEM((RET_HEADS, blk, LANES), jnp.float32),
            pltpu.VMEM((RET_HEADS, blk, LANES), jnp.float32),
            pltpu.VMEM((RET_HEADS, 8, LANES), jnp.float32),
        ],
        compiler_params=pltpu.CompilerParams(
            dimension_semantics=("arbitrary", "arbitrary"), vmem_limit_bytes=VMEM_LIMIT),
        name="ret_output_stage",
    )(log_gamma, rq, rk, rv, rg, mixed_fox, wo, gain, x)


def _layer(h, pre_gain, w_in, forget_bias, w_out, post_gain, cos_t, sin_t, log_gamma):
    batch, seq, _ = h.shape
    x2 = h.reshape(batch * seq, D_MODEL)
    wb = w_in.astype(jnp.bfloat16)
    fox_cols = wb[:, :4 * FOX_WIDTH].reshape(D_MODEL, 4, FOX_WIDTH)
    ret_cols = wb[:, 4 * FOX_WIDTH + FOX_HEADS:].reshape(D_MODEL, 4, RET_WIDTH)
    w8 = jnp.concatenate([fox_cols, ret_cols], axis=1).transpose(1, 0, 2)
    wf = jnp.pad(jnp.repeat(wb[:, 4 * FOX_WIDTH:4 * FOX_WIDTH + FOX_HEADS], 3, axis=1),
                 ((0, 0), (0, LANES - 3 * FOX_HEADS)))
    fbias = jnp.pad(jnp.repeat(forget_bias.astype(jnp.float32), 3), (0, LANES - 3 * FOX_HEADS))[None, :]

    fq, fk, fv, fg, rq, rk, rv, rg = _input_stage(
        x2, pre_gain[None, :], w8, wf, fbias, cos_t, sin_t, seq)

    def seq3(a):
        return a.reshape(batch, seq, a.shape[-1])

    mixed_fox = _fox_stage(fq, seq3(fk), seq3(fv), seq3(fg))
    return _ret_output_stage(log_gamma, seq3(rq), seq3(rk), seq3(rv), seq3(rg), mixed_fox,
                             w_out.astype(jnp.bfloat16), post_gain[None, :], h)


def kernel(x, pre_norm_gain, w_in, fox_forget_bias, w_out, post_norm_gain):
    seq = x.shape[1]
    cos_t, sin_t = _rope_tables(seq)
    log_gamma = jnp.log1p(
        -jnp.exp(jnp.linspace(math.log(1.0 / 32), math.log(1.0 / 512), RET_HEADS))).astype(jnp.float32)
    h = x
    for layer in range(pre_norm_gain.shape[0]):
        h = _layer(h, pre_norm_gain[layer], w_in[layer], fox_forget_bias[layer], w_out[layer],
                   post_norm_gain[layer], cos_t, sin_t, log_gamma)
    return h
```

```python
import functools
import math

import jax
import jax.numpy as jnp
from jax import lax
from jax.experimental import pallas as pl
from jax.experimental.pallas import tpu as pltpu

D_MODEL = 1024
CHUNK = 64
FOX_HEADS = 8
FOX_HEAD_DIM = 64
FOX_WIDTH = FOX_HEADS * FOX_HEAD_DIM
RET_HEADS = 4
RET_HEAD_DIM = 128
RET_WIDTH = RET_HEADS * RET_HEAD_DIM
ROPE_BASE = 10000.0
NORM_EPS = 1e-6
GN_EPS = 1e-5

LANES = 128
LOG2E = math.log2(math.e)
NEG_BIG = -1e30

IN_ROWS = 512
FOX_BLOCK = 512
FOX_KEYS = 256
FOX_STEP_HEADS = 4
RET_BLOCK = 256
VMEM_LIMIT = 48 * 1024 * 1024


def _dot(a, b):
    return jnp.dot(a, b, preferred_element_type=jnp.float32)


def _dot_nt(a, b):
    return lax.dot_general(a, b, (((1,), (1,)), ((), ())), preferred_element_type=jnp.float32)


def _dot_tn(a, b):
    return lax.dot_general(a, b, (((0,), (0,)), ((), ())), preferred_element_type=jnp.float32)


def _rope_table_kernel(inv_ref, cos_ref, sin_ref):
    rows = cos_ref.shape[0]
    pos = (lax.broadcasted_iota(jnp.int32, (rows, LANES), 0) + pl.program_id(0) * rows).astype(jnp.float32)
    lane = lax.broadcasted_iota(jnp.int32, (rows, LANES), 1)
    ang = pos * inv_ref[...]
    cos_ref[...] = jnp.cos(ang)
    sin_ref[...] = jnp.where(lane < LANES // 2, -1.0, 1.0) * jnp.sin(ang)


def _rope_tables(seq):
    half = RET_HEAD_DIM // 2
    inv = 1.0 / (ROPE_BASE ** (jnp.arange(half, dtype=jnp.float32) / half))
    inv2 = jnp.concatenate([inv, inv])[None, :]
    rows = 512
    return pl.pallas_call(
        _rope_table_kernel,
        grid=(seq // rows,),
        in_specs=[pl.BlockSpec((1, LANES), lambda i: (0, 0))],
        out_specs=[pl.BlockSpec((rows, LANES), lambda i: (i, 0))] * 2,
        out_shape=[jax.ShapeDtypeStruct((seq, LANES), jnp.float32)] * 2,
        name="rope_tables",
    )(inv2)


def _input_kernel(x_ref, g_ref, w_ref, wf_ref, fb_ref, cos_ref, sin_ref,
                  fq_ref, fk_ref, fv_ref, fg_ref, rq_ref, rk_ref, rv_ref, rg_ref,
                  carry_ref, z_ref, *, seq_tiles):
    rows = x_ref.shape[0]

    @pl.when(pl.program_id(0) % seq_tiles == 0)
    def _():
        carry_ref[...] = jnp.zeros_like(carry_ref)

    x = x_ref[...]
    ms = jnp.mean(x * x, axis=-1, keepdims=True)
    u = (x * lax.rsqrt(ms + NORM_EPS) * g_ref[...]).astype(jnp.bfloat16)

    def silu(z):
        return z * (1.0 / (1.0 + jnp.exp(-z)))

    def rotary(z):
        cos = cos_ref[...]
        sin = sin_ref[...]
        heads = []
        for h in range(RET_HEADS):
            zh = z[:, h * LANES:(h + 1) * LANES]
            heads.append(zh * cos + pltpu.roll(zh, LANES // 2, axis=1) * sin)
        return jnp.concatenate(heads, axis=1)

    def store_fk(z):
        fk = z.astype(jnp.bfloat16)
        for p in range(FOX_HEADS // 2):
            fk_ref[:, 2 * p * LANES:(2 * p + 1) * LANES] = fk[:, p * LANES:(p + 1) * LANES]

    def store(ref, fn):
        def epilogue(z):
            ref[...] = fn(z).astype(jnp.bfloat16)
        return epilogue

    stages = [
        (3, store(fg_ref, silu)),
        (7, store(rg_ref, silu)),
        (4, store(rq_ref, rotary)),
        (5, store(rk_ref, lambda z: rotary(z) * (RET_HEAD_DIM ** -0.5))),
        (0, store(fq_ref, lambda z: jnp.transpose(z * (FOX_HEAD_DIM ** -0.5 * LOG2E)))),
        (1, store_fk),
        (2, store(fv_ref, lambda z: z)),
        (6, store(rv_ref, lambda z: z)),
    ]

    used = 3 * FOX_HEADS
    f = jnp.transpose(_dot(u, wf_ref[...]) + fb_ref[...])[:used, :]
    chain = {"y": -(jnp.maximum(-f, 0.0) + jnp.log1p(jnp.exp(-jnp.abs(f)))) * LOG2E}
    pos = lax.broadcasted_iota(jnp.int32, (used, rows), 1)
    shifts = [1 << b for b in range(rows.bit_length() - 1)]

    def scan_steps(some):
        def piece():
            y = chain["y"]
            for shift in some:
                y = y + jnp.where(pos >= shift, pltpu.roll(y, shift, axis=1), 0.0)
            chain["y"] = y
        return piece

    def add_carry():
        y = chain["y"] + carry_ref[:, 0:1]
        carry_ref[...] = jnp.broadcast_to(y[:, rows - 1:rows], carry_ref.shape)
        chain["y"] = y

    def split_pieces():
        neg_c = -chain["y"]
        hi = neg_c.astype(jnp.bfloat16).astype(jnp.float32)
        rest = neg_c - hi
        mid = rest.astype(jnp.bfloat16).astype(jnp.float32)
        lo = rest - mid
        copy = lax.broadcasted_iota(jnp.int32, neg_c.shape, 0) % 3
        piece = jnp.where(copy == 0, hi, jnp.where(copy == 1, mid, lo))
        padded = jnp.concatenate([piece, jnp.zeros((LANES - used, rows), jnp.float32)], axis=0)
        chain["piece"] = jnp.transpose(padded)

    def store_pieces():
        piece = chain["piece"]
        lane = lax.broadcasted_iota(jnp.int32, piece.shape, 1)
        pair_lanes = 2 * 3
        for p in range(FOX_HEADS // 2):
            shifted = piece if p == 0 else pltpu.roll(piece, LANES - pair_lanes * p, axis=1)
            extra = jnp.where(lane < pair_lanes, shifted, 0.0).astype(jnp.bfloat16)
            fk_ref[:, (2 * p + 1) * LANES:(2 * p + 2) * LANES] = extra

    chain_pieces = [scan_steps(shifts[0:3]), scan_steps(shifts[3:6]), scan_steps(shifts[6:]), add_carry,
                    split_pieces, store_pieces]

    for n, (segment, _) in enumerate(stages):
        z_ref[n % 2] = _dot(u, w_ref[segment])
        if n > 0:
            stages[n - 1][1](z_ref[(n - 1) % 2])
            if n - 1 < len(chain_pieces):
                chain_pieces[n - 1]()
    stages[-1][1](z_ref[(len(stages) - 1) % 2])


def _input_stage(x2, gain, w8, wf, fbias, cos_t, sin_t, seq):
    n = x2.shape[0]
    rows = IN_ROWS
    seq_tiles = seq // rows
    act = jax.ShapeDtypeStruct((n, FOX_WIDTH), jnp.bfloat16)
    act_spec = pl.BlockSpec((rows, FOX_WIDTH), lambda i: (i, 0))
    wide = jax.ShapeDtypeStruct((n, 2 * FOX_WIDTH), jnp.bfloat16)
    wide_spec = pl.BlockSpec((rows, 2 * FOX_WIDTH), lambda i: (i, 0))
    tab_spec = pl.BlockSpec((rows, LANES), lambda i: (i % seq_tiles, 0))
    return pl.pallas_call(
        functools.partial(_input_kernel, seq_tiles=seq_tiles),
        grid=(n // rows,),
        in_specs=[
            pl.BlockSpec((rows, D_MODEL), lambda i: (i, 0)),
            pl.BlockSpec((1, D_MODEL), lambda i: (0, 0)),
            pl.BlockSpec((8, D_MODEL, FOX_WIDTH), lambda i: (0, 0, 0)),
            pl.BlockSpec((D_MODEL, LANES), lambda i: (0, 0)),
            pl.BlockSpec((1, LANES), lambda i: (0, 0)),
            tab_spec,
            tab_spec,
        ],
        out_specs=[pl.BlockSpec((FOX_WIDTH, rows), lambda i: (0, i)), wide_spec] + [act_spec] * 6,
        out_shape=[jax.ShapeDtypeStruct((FOX_WIDTH, n), jnp.bfloat16), wide] + [act] * 6,
        scratch_shapes=[
            pltpu.VMEM((3 * FOX_HEADS, LANES), jnp.float32),
            pltpu.VMEM((2, rows, FOX_WIDTH), jnp.float32),
        ],
        compiler_params=pltpu.CompilerParams(dimension_semantics=("arbitrary",), vmem_limit_bytes=VMEM_LIMIT),
        name="input_stage",
    )(x2, gain, w8, wf, fbias, cos_t, sin_t)


ONES_ROWS = 16


def _fox_kernel(q_ref, k_ref, v_ref, g_ref, o_ref, vt_ref, s_ref, bias_ref):
    blk = g_ref.shape[1] // 2
    i = pl.program_id(2)
    dh = FOX_HEAD_DIM
    heads = vt_ref.shape[0]
    sub = vt_ref.shape[3]
    subs = blk // sub
    half = blk // 2
    slots = s_ref.shape[0]

    @pl.when(i == 0)
    def _():
        key = lax.broadcasted_iota(jnp.int32, bias_ref.shape, 0)
        query = lax.broadcasted_iota(jnp.int32, bias_ref.shape, 1)
        bias_ref[...] = jnp.where(key <= query, 0.0, NEG_BIG)
        ones = jnp.ones((ONES_ROWS, sub), jnp.bfloat16)
        for jb in range(v_ref.shape[1] // sub):
            vt = jnp.transpose(v_ref[0, jb * sub:(jb + 1) * sub, :].astype(jnp.float32))
            for h in range(heads):
                vt_ref[h, jb, 0:dh, :] = vt[h * dh:(h + 1) * dh].astype(jnp.bfloat16)
                vt_ref[h, jb, dh:dh + ONES_ROWS, :] = ones

    def query_operands(qb):
        per_head = []
        for h in range(heads):
            pair, second = divmod(h, 2)
            qt = q_ref[pair * LANES:(pair + 1) * LANES, qb * blk:(qb + 1) * blk].astype(jnp.float32)
            row = lax.broadcasted_iota(jnp.int32, qt.shape, 0)
            own = jnp.where((row >= second * dh) & (row < (second + 1) * dh), qt, 0.0)
            ones = jnp.where((row >= 3 * second) & (row < 3 * second + 3), 1.0, 0.0)
            per_head.append(jnp.concatenate([own, ones], axis=0).astype(jnp.bfloat16))
        return per_head

    qts = [query_operands(0), query_operands(1)]

    def causal(s, t, lo):
        assert t * sub == lo
        return s + bias_ref[:, 0:s.shape[1]]

    def produce(unit, slot):
        j, t, h, qb, lo, mask = unit
        start = pl.multiple_of(j * blk + t * sub, sub)
        k_cols = slice(h // 2 * 2 * LANES, (h // 2 + 1) * 2 * LANES)
        s = _dot(k_ref[0, pl.ds(start, sub), k_cols], qts[qb][h][:, lo:])
        if mask == "produce":
            s = causal(s, t, lo)
        s_ref[slot, :, lo:] = s
        return jnp.max(s, axis=0, keepdims=True)

    def consume(state, unit, slot, col_max):
        j, t, h, qb, lo, mask = unit
        m, acc = state[qb][h]
        s = s_ref[slot, :, lo:]
        if mask == "consume":
            s = causal(s, t, lo)
            col_max = jnp.max(s, axis=0, keepdims=True)
        m_new = jnp.maximum(m[:, lo:], col_max)
        p = jnp.exp2(s - m_new).astype(jnp.bfloat16)
        acc_new = jnp.exp2(m[:, lo:] - m_new) * acc[:, lo:] + _dot(vt_ref[h, j * subs + t], p)
        if lo:
            m_new = jnp.concatenate([m[:, :lo], m_new], axis=1)
            acc_new = jnp.concatenate([acc[:, :lo], acc_new], axis=1)
        state[qb][h] = (m_new, acc_new)

    def finish(state, qb):
        out_t = jnp.concatenate([acc[0:dh] / acc[dh:dh + 1] for _, acc in state[qb]], axis=0)
        rows = slice(qb * blk, (qb + 1) * blk)
        o_ref[0, rows, :] = (jnp.transpose(out_t) * g_ref[0, rows, :].astype(jnp.float32)).astype(o_ref.dtype)

    def run(state, units, first_max, following=(), after=None):
        pending = dict(enumerate(first_max))
        stream = list(units) + list(following)
        for n, unit in enumerate(units):
            if n + 2 < len(stream):
                pending[n + 2] = produce(stream[n + 2], (n + 2) % slots)
            consume(state, unit, n % slots, pending.pop(n))
            if after and n in after:
                after[n]()
        return tuple(pending[n] for n in sorted(pending))

    def shared(j):
        return [(j, t, h, qb, 0, None) for t in range(subs) for qb in range(2) for h in range(heads)]

    def shared_blocks(jj, carry):
        state, first_max = carry
        state = [list(per_block) for per_block in state]
        first_max = run(state, shared(2 * jj) + shared(2 * jj + 1), first_max, following=shared(2 * jj + 2)[:2])
        return tuple(tuple(per_block) for per_block in state), first_max

    init = tuple(
        tuple((jnp.full((1, blk), NEG_BIG, jnp.float32), jnp.zeros((dh + ONES_ROWS, blk), jnp.float32))
              for _ in range(heads)) for _ in range(2))
    assert len(shared(0)) % slots == 0
    first_max = (produce(shared(0)[0], 0), produce(shared(0)[1], 1))
    state, first_max = lax.fori_loop(0, i, shared_blocks, (init, first_max))

    ja, jb = 2 * i, 2 * i + 1
    every = range(heads)
    tail = (
        [(ja, 0, h, 0, 0, "consume" if h < 2 else "produce") for h in every]
        + [(ja, 0, h, 1, 0, None) for h in every]
        + [(ja, 1, h, 0, half, "produce") for h in every]
        + [(ja, 1, h, 1, 0, None) for h in every]
        + [(jb, 0, h, 1, 0, "produce") for h in every]
        + [(jb, 1, h, 1, half, "produce") for h in every]
    )
    state = [list(per_block) for per_block in state]
    run(state, tail, first_max, after={3 * heads - 1: lambda: finish(state, 0)})
    finish(state, 1)


def _fox_stage(fq_t, fk, fv, fg):
    batch, seq, _ = fk.shape
    blk = FOX_BLOCK
    heads = FOX_STEP_HEADS
    width = heads * FOX_HEAD_DIM
    groups = FOX_HEADS // heads
    steps = seq // (2 * blk)
    tile = pl.BlockSpec((1, 2 * blk, width), lambda b, p, i: (b, i, p))
    return pl.pallas_call(
        _fox_kernel,
        grid=(batch, groups, steps),
        in_specs=[
            pl.BlockSpec((width, 2 * blk), lambda b, p, i: (p, b * steps + i)),
            pl.BlockSpec((1, seq, 2 * width), lambda b, p, i: (b, 0, p)),
            pl.BlockSpec((1, seq, width), lambda b, p, i: (b, 0, p)),
            tile,
        ],
        out_specs=tile,
        out_shape=jax.ShapeDtypeStruct((batch, seq, FOX_WIDTH), jnp.bfloat16),
        scratch_shapes=[
            pltpu.VMEM((heads, seq // FOX_KEYS, FOX_HEAD_DIM + ONES_ROWS, FOX_KEYS), jnp.bfloat16),
            pltpu.VMEM((4 * blk // FOX_KEYS, FOX_KEYS, blk), jnp.float32),
            pltpu.VMEM((FOX_KEYS, blk), jnp.float32),
        ],
        compiler_params=pltpu.CompilerParams(
            dimension_semantics=("arbitrary", "arbitrary", "arbitrary"), vmem_limit_bytes=VMEM_LIMIT),
        name="fox_stage",
    )(fq_t, fk, fv, fg)


def _ret_kernel(lg_ref, q_ref, k_ref, v_ref, g_ref, mf_ref, wo_ref, gain_ref, x_ref, o_ref,
                mixed_ref, state_ref, decay_ref, qdec_ref, kdec_ref, sdec_ref):
    blk = q_ref.shape[1]

    @pl.when((pl.program_id(0) == 0) & (pl.program_id(1) == 0))
    def _():
        row = lax.broadcasted_iota(jnp.int32, (blk, blk), 0)
        col = lax.broadcasted_iota(jnp.int32, (blk, blk), 1)
        dist = jnp.abs(row - col).astype(jnp.float32)
        pos = lax.broadcasted_iota(jnp.int32, (blk, LANES), 0).astype(jnp.float32)
        for h in range(RET_HEADS):
            lg = lg_ref[h]
            decay_ref[h] = jnp.where(col // CHUNK <= row // CHUNK, jnp.exp(lg * dist), 0.0)
            qdec_ref[h] = jnp.exp(lg * (pos + 1.0))
            kdec_ref[h] = jnp.exp(lg * (blk - 1.0 - pos))
            sdec_ref[h] = jnp.exp(lg * jnp.full((8, LANES), blk, jnp.float32))

    t = pl.program_id(1)

    @pl.when(t == 0)
    def _():
        state_ref[...] = jnp.zeros_like(state_ref)
        mixed_ref[...] = jnp.zeros_like(mixed_ref)

    previous = jnp.concatenate([mf_ref[0], mixed_ref[(t + 1) % 2]], axis=1)
    projected = []
    for h in range(RET_HEADS):
        cols = slice(h * LANES, (h + 1) * LANES)
        q = q_ref[0, :, cols]
        k = k_ref[0, :, cols]
        v = v_ref[0, :, cols]
        state = state_ref[h]

        scores = _dot_nt(q, k)
        cross = _dot(q, state.astype(jnp.bfloat16))
        if h % 2 == 0:
            half_cols = slice(h // 2 * (D_MODEL // 2), (h // 2 + 1) * (D_MODEL // 2))
            projected.append(_dot(previous, wo_ref[:, half_cols]))
        inner = _dot((scores * decay_ref[h]).astype(jnp.bfloat16), v)
        kd = (k.astype(jnp.float32) * kdec_ref[h]).astype(jnp.bfloat16)
        state_ref[h] = state * sdec_ref[h, 0:1, :] + _dot_tn(kd, v)

        o = inner + cross * qdec_ref[h]
        mu = jnp.mean(o, axis=-1, keepdims=True)
        d = o - mu
        var = jnp.mean(d * d, axis=-1, keepdims=True)
        y = d * lax.rsqrt(var + GN_EPS)
        mixed_ref[t % 2, :, cols] = (y * g_ref[0, :, cols].astype(jnp.float32)).astype(mixed_ref.dtype)

    out = jnp.concatenate(projected, axis=1)
    ms = jnp.mean(out * out, axis=-1, keepdims=True)
    o_ref[0] = x_ref[0] + out * lax.rsqrt(ms + NORM_EPS) * gain_ref[...]


def _ret_output_stage(log_gamma, rq, rk, rv, rg, mixed_fox, wo, gain, x):
    batch, seq, _ = rq.shape
    blk = RET_BLOCK
    steps = seq // blk
    tile = pl.BlockSpec((1, blk, RET_WIDTH), lambda b, t: (b, jnp.minimum(t, steps - 1), 0))
    lagged = pl.BlockSpec((1, blk, FOX_WIDTH), lambda b, t: (b, jnp.maximum(t - 1, 0), 0))
    wide = pl.BlockSpec((1, blk, D_MODEL), lambda b, t: (b, jnp.maximum(t - 1, 0), 0))
    return pl.pallas_call(
        _ret_kernel,
        grid=(batch, steps + 1),
        in_specs=[
            pl.BlockSpec(memory_space=pltpu.SMEM), tile, tile, tile, tile, lagged,
            pl.BlockSpec((FOX_WIDTH + RET_WIDTH, D_MODEL), lambda b, t: (0, 0)),
            pl.BlockSpec((1, D_MODEL), lambda b, t: (0, 0)),
            wide,
        ],
        out_specs=wide,
        out_shape=jax.ShapeDtypeStruct((batch, seq, D_MODEL), jnp.float32),
        scratch_shapes=[
            pltpu.VMEM((2, blk, RET_WIDTH), jnp.bfloat16),
            pltpu.VMEM((RET_HEADS, RET_HEAD_DIM, RET_HEAD_DIM), jnp.float32),
            pltpu.VMEM((RET_HEADS, blk, blk), jnp.float32),
            pltpu.VMEM((RET_HEADS, blk, LANES), jnp.float32),
            pltpu.VMEM((RET_HEADS, blk, LANES), jnp.float32),
            pltpu.VMEM((RET_HEADS, 8, LANES), jnp.float32),
        ],
        compiler_params=pltpu.CompilerParams(
            dimension_semantics=("arbitrary", "arbitrary"), vmem_limit_bytes=VMEM_LIMIT),
        name="ret_output_stage",
    )(log_gamma, rq, rk, rv, rg, mixed_fox, wo, gain, x)


def _layer(h, pre_gain, w_in, forget_bias, w_out, post_gain, cos_t, sin_t, log_gamma):
    batch, seq, _ = h.shape
    x2 = h.reshape(batch * seq, D_MODEL)
    wb = w_in.astype(jnp.bfloat16)
    fox_cols = wb[:, :4 * FOX_WIDTH].reshape(D_MODEL, 4, FOX_WIDTH)
    ret_cols = wb[:, 4 * FOX_WIDTH + FOX_HEADS:].reshape(D_MODEL, 4, RET_WIDTH)
    w8 = jnp.concatenate([fox_cols, ret_cols], axis=1).transpose(1, 0, 2)
    wf = jnp.pad(jnp.repeat(wb[:, 4 * FOX_WIDTH:4 * FOX_WIDTH + FOX_HEADS], 3, axis=1),
                 ((0, 0), (0, LANES - 3 * FOX_HEADS)))
    fbias = jnp.pad(jnp.repeat(forget_bias.astype(jnp.float32), 3), (0, LANES - 3 * FOX_HEADS))[None, :]

    fq, fk, fv, fg, rq, rk, rv, rg = _input_stage(
        x2, pre_gain[None, :], w8, wf, fbias, cos_t, sin_t, seq)

    def seq3(a):
        return a.reshape(batch, seq, a.shape[-1])

    mixed_fox = _fox_stage(fq, seq3(fk), seq3(fv), seq3(fg))
    return _ret_output_stage(log_gamma, seq3(rq), seq3(rk), seq3(rv), seq3(rg), mixed_fox,
                             w_out.astype(jnp.bfloat16), post_gain[None, :], h)


def kernel(x, pre_norm_gain, w_in, fox_forget_bias, w_out, post_norm_gain):
    seq = x.shape[1]
    cos_t, sin_t = _rope_tables(seq)
    log_gamma = jnp.log1p(
        -jnp.exp(jnp.linspace(math.log(1.0 / 32), math.log(1.0 / 512), RET_HEADS))).astype(jnp.float32)
    h = x
    for layer in range(pre_norm_gain.shape[0]):
        h = _layer(h, pre_norm_gain[layer], w_in[layer], fox_forget_bias[layer], w_out[layer],
                   post_norm_gain[layer], cos_t, sin_t, log_gamma)
    return h
```

```python
import functools
import math

import jax
import jax.numpy as jnp
from jax import lax
from jax.experimental import pallas as pl
from jax.experimental.pallas import tpu as pltpu

D_MODEL = 1024
CHUNK = 64
FOX_HEADS = 8
FOX_HEAD_DIM = 64
FOX_WIDTH = FOX_HEADS * FOX_HEAD_DIM
RET_HEADS = 4
RET_HEAD_DIM = 128
RET_WIDTH = RET_HEADS * RET_HEAD_DIM
ROPE_BASE = 10000.0
NORM_EPS = 1e-6
GN_EPS = 1e-5

LANES = 128
LOG2E = math.log2(math.e)
NEG_BIG = -1e30

IN_ROWS = 512
FOX_BLOCK = 512
FOX_KEYS = 256
FOX_STEP_HEADS = 4
RET_BLOCK = 256
VMEM_LIMIT = 48 * 1024 * 1024


def _dot(a, b):
    return jnp.dot(a, b, preferred_element_type=jnp.float32)


def _dot_nt(a, b):
    return lax.dot_general(a, b, (((1,), (1,)), ((), ())), preferred_element_type=jnp.float32)


def _dot_tn(a, b):
    return lax.dot_general(a, b, (((0,), (0,)), ((), ())), preferred_element_type=jnp.float32)


def _rope_table_kernel(inv_ref, cos_ref, sin_ref):
    rows = cos_ref.shape[0]
    pos = (lax.broadcasted_iota(jnp.int32, (rows, LANES), 0) + pl.program_id(0) * rows).astype(jnp.float32)
    lane = lax.broadcasted_iota(jnp.int32, (rows, LANES), 1)
    ang = pos * inv_ref[...]
    cos_ref[...] = jnp.cos(ang)
    sin_ref[...] = jnp.where(lane < LANES // 2, -1.0, 1.0) * jnp.sin(ang)


def _rope_tables(seq):
    half = RET_HEAD_DIM // 2
    inv = 1.0 / (ROPE_BASE ** (jnp.arange(half, dtype=jnp.float32) / half))
    inv2 = jnp.concatenate([inv, inv])[None, :]
    rows = 512
    return pl.pallas_call(
        _rope_table_kernel,
        grid=(seq // rows,),
        in_specs=[pl.BlockSpec((1, LANES), lambda i: (0, 0))],
        out_specs=[pl.BlockSpec((rows, LANES), lambda i: (i, 0))] * 2,
        out_shape=[jax.ShapeDtypeStruct((seq, LANES), jnp.float32)] * 2,
        name="rope_tables",
    )(inv2)


def _input_kernel(x_ref, g_ref, w_ref, wf_ref, fb_ref, cos_ref, sin_ref,
                  fq_ref, fk_ref, fv_ref, fg_ref, rq_ref, rk_ref, rv_ref, rg_ref,
                  carry_ref, z_ref, *, seq_tiles):
    rows = x_ref.shape[0]

    @pl.when(pl.program_id(0) % seq_tiles == 0)
    def _():
        carry_ref[...] = jnp.zeros_like(carry_ref)

    x = x_ref[...]
    ms = jnp.mean(x * x, axis=-1, keepdims=True)
    u = (x * lax.rsqrt(ms + NORM_EPS) * g_ref[...]).astype(jnp.bfloat16)

    def silu(z):
        return z * (1.0 / (1.0 + jnp.exp(-z)))

    def rotary(z):
        cos = cos_ref[...]
        sin = sin_ref[...]
        heads = []
        for h in range(RET_HEADS):
            zh = z[:, h * LANES:(h + 1) * LANES]
            heads.append(zh * cos + pltpu.roll(zh, LANES // 2, axis=1) * sin)
        return jnp.concatenate(heads, axis=1)

    def store_fk(z):
        fk = z.astype(jnp.bfloat16)
        for p in range(FOX_HEADS // 2):
            fk_ref[:, 2 * p * LANES:(2 * p + 1) * LANES] = fk[:, p * LANES:(p + 1) * LANES]

    def store(ref, fn):
        def epilogue(z):
            ref[...] = fn(z).astype(jnp.bfloat16)
        return epilogue

    stages = [
        (3, store(fg_ref, silu)),
        (7, store(rg_ref, silu)),
        (4, store(rq_ref, rotary)),
        (5, store(rk_ref, lambda z: rotary(z) * (RET_HEAD_DIM ** -0.5))),
        (0, store(fq_ref, lambda z: jnp.transpose(z * (FOX_HEAD_DIM ** -0.5 * LOG2E)))),
        (1, store_fk),
        (2, store(fv_ref, lambda z: z)),
        (6, store(rv_ref, lambda z: z)),
    ]

    used = 3 * FOX_HEADS
    f = jnp.transpose(_dot(u, wf_ref[...]) + fb_ref[...])[:used, :]
    chain = {"y": -(jnp.maximum(-f, 0.0) + jnp.log1p(jnp.exp(-jnp.abs(f)))) * LOG2E}
    pos = lax.broadcasted_iota(jnp.int32, (used, rows), 1)
    shifts = [1 << b for b in range(rows.bit_length() - 1)]

    def scan_steps(some):
        def piece():
            y = chain["y"]
            for shift in some:
                y = y + jnp.where(pos >= shift, pltpu.roll(y, shift, axis=1), 0.0)
            chain["y"] = y
        return piece

    def add_carry():
        y = chain["y"] + carry_ref[:, 0:1]
        carry_ref[...] = jnp.broadcast_to(y[:, rows - 1:rows], carry_ref.shape)
        chain["y"] = y

    def split_pieces():
        neg_c = -chain["y"]
        hi = neg_c.astype(jnp.bfloat16).astype(jnp.float32)
        rest = neg_c - hi
        mid = rest.astype(jnp.bfloat16).astype(jnp.float32)
        lo = rest - mid
        copy = lax.broadcasted_iota(jnp.int32, neg_c.shape, 0) % 3
        piece = jnp.where(copy == 0, hi, jnp.where(copy == 1, mid, lo))
        padded = jnp.concatenate([piece, jnp.zeros((LANES - used, rows), jnp.float32)], axis=0)
        chain["piece"] = jnp.transpose(padded)

    def store_pieces():
        piece = chain["piece"]
        lane = lax.broadcasted_iota(jnp.int32, piece.shape, 1)
        pair_lanes = 2 * 3
        for p in range(FOX_HEADS // 2):
            shifted = piece if p == 0 else pltpu.roll(piece, LANES - pair_lanes * p, axis=1)
            extra = jnp.where(lane < pair_lanes, shifted, 0.0).astype(jnp.bfloat16)
            fk_ref[:, (2 * p + 1) * LANES:(2 * p + 2) * LANES] = extra

    chain_pieces = [scan_steps(shifts[0:3]), scan_steps(shifts[3:6]), scan_steps(shifts[6:]), add_carry,
                    split_pieces, store_pieces]

    for n, (segment, _) in enumerate(stages):
        z_ref[n % 2] = _dot(u, w_ref[segment])
        if n > 0:
            stages[n - 1][1](z_ref[(n - 1) % 2])
            if n - 1 < len(chain_pieces):
                chain_pieces[n - 1]()
    stages[-1][1](z_ref[(len(stages) - 1) % 2])


def _input_stage(x2, gain, w8, wf, fbias, cos_t, sin_t, seq):
    n = x2.shape[0]
    rows = IN_ROWS
    seq_tiles = seq // rows
    act = jax.ShapeDtypeStruct((n, FOX_WIDTH), jnp.bfloat16)
    act_spec = pl.BlockSpec((rows, FOX_WIDTH), lambda i: (i, 0))
    wide = jax.ShapeDtypeStruct((n, 2 * FOX_WIDTH), jnp.bfloat16)
    wide_spec = pl.BlockSpec((rows, 2 * FOX_WIDTH), lambda i: (i, 0))
    tab_spec = pl.BlockSpec((rows, LANES), lambda i: (i % seq_tiles, 0))
    return pl.pallas_call(
        functools.partial(_input_kernel, seq_tiles=seq_tiles),
        grid=(n // rows,),
        in_specs=[
            pl.BlockSpec((rows, D_MODEL), lambda i: (i, 0)),
            pl.BlockSpec((1, D_MODEL), lambda i: (0, 0)),
            pl.BlockSpec((8, D_MODEL, FOX_WIDTH), lambda i: (0, 0, 0)),
            pl.BlockSpec((D_MODEL, LANES), lambda i: (0, 0)),
            pl.BlockSpec((1, LANES), lambda i: (0, 0)),
            tab_spec,
            tab_spec,
        ],
        out_specs=[pl.BlockSpec((FOX_WIDTH, rows), lambda i: (0, i)), wide_spec] + [act_spec] * 6,
        out_shape=[jax.ShapeDtypeStruct((FOX_WIDTH, n), jnp.bfloat16), wide] + [act] * 6,
        scratch_shapes=[
            pltpu.VMEM((3 * FOX_HEADS, LANES), jnp.float32),
            pltpu.VMEM((2, rows, FOX_WIDTH), jnp.float32),
        ],
        compiler_params=pltpu.CompilerParams(dimension_semantics=("arbitrary",), vmem_limit_bytes=VMEM_LIMIT),
        name="input_stage",
    )(x2, gain, w8, wf, fbias, cos_t, sin_t)


ONES_ROWS = 16


def _fox_kernel(q_ref, k_ref, v_ref, g_ref, o_ref, vt_ref, s_ref, bias_ref):
    blk = g_ref.shape[1] // 2
    i = pl.program_id(2)
    dh = FOX_HEAD_DIM
    heads = vt_ref.shape[0]
    sub = vt_ref.shape[3]
    subs = blk // sub
    half = blk // 2
    slots = s_ref.shape[0]

    @pl.when(i == 0)
    def _():
        key = lax.broadcasted_iota(jnp.int32, bias_ref.shape, 0)
        query = lax.broadcasted_iota(jnp.int32, bias_ref.shape, 1)
        bias_ref[...] = jnp.where(key <= query, 0.0, NEG_BIG)
        ones = jnp.ones((ONES_ROWS, sub), jnp.bfloat16)
        for jb in range(v_ref.shape[1] // sub):
            vt = jnp.transpose(v_ref[0, jb * sub:(jb + 1) * sub, :].astype(jnp.float32))
            for h in range(heads):
                vt_ref[h, jb, 0:dh, :] = vt[h * dh:(h + 1) * dh].astype(jnp.bfloat16)
                vt_ref[h, jb, dh:dh + ONES_ROWS, :] = ones

    def query_operands(qb):
        per_head = []
        for h in range(heads):
            pair, second = divmod(h, 2)
            qt = q_ref[pair * LANES:(pair + 1) * LANES, qb * blk:(qb + 1) * blk].astype(jnp.float32)
            row = lax.broadcasted_iota(jnp.int32, qt.shape, 0)
            own = jnp.where((row >= second * dh) & (row < (second + 1) * dh), qt, 0.0)
            ones = jnp.where((row >= 3 * second) & (row < 3 * second + 3), 1.0, 0.0)
            per_head.append(jnp.concatenate([own, ones], axis=0).astype(jnp.bfloat16))
        return per_head

    qts = [query_operands(0), query_operands(1)]

    def causal(s, t, lo):
        assert t * sub == lo
        return s + bias_ref[:, 0:s.shape[1]]

    def produce(unit, slot):
        j, t, h, qb, lo, mask = unit
        start = pl.multiple_of(j * blk + t * sub, sub)
        k_cols = slice(h // 2 * 2 * LANES, (h // 2 + 1) * 2 * LANES)
        s = _dot(k_ref[0, pl.ds(start, sub), k_cols], qts[qb][h][:, lo:])
        if mask == "produce":
            s = causal(s, t, lo)
        s_ref[slot, :, lo:] = s
        return jnp.max(s, axis=0, keepdims=True)

    def consume(state, unit, slot, col_max):
        j, t, h, qb, lo, mask = unit
        m, acc = state[qb][h]
        s = s_ref[slot, :, lo:]
        if mask == "consume":
            s = causal(s, t, lo)
            col_max = jnp.max(s, axis=0, keepdims=True)
        m_new = jnp.maximum(m[:, lo:], col_max)
        p = jnp.exp2(s - m_new).astype(jnp.bfloat16)
        acc_new = jnp.exp2(m[:, lo:] - m_new) * acc[:, lo:] + _dot(vt_ref[h, j * subs + t], p)
        if lo:
            m_new = jnp.concatenate([m[:, :lo], m_new], axis=1)
            acc_new = jnp.concatenate([acc[:, :lo], acc_new], axis=1)
        state[qb][h] = (m_new, acc_new)

    def finish(state, qb):
        out_t = jnp.concatenate([acc[0:dh] / acc[dh:dh + 1] for _, acc in state[qb]], axis=0)
        rows = slice(qb * blk, (qb + 1) * blk)
        o_ref[0, rows, :] = (jnp.transpose(out_t) * g_ref[0, rows, :].astype(jnp.float32)).astype(o_ref.dtype)

    def run(state, units, first_max, following=(), after=None):
        pending = dict(enumerate(first_max))
        stream = list(units) + list(following)
        for n, unit in enumerate(units):
            if n + 2 < len(stream):
                pending[n + 2] = produce(stream[n + 2], (n + 2) % slots)
            consume(state, unit, n % slots, pending.pop(n))
            if after and n in after:
                after[n]()
        return tuple(pending[n] for n in sorted(pending))

    def shared(j):
        return [(j, t, h, qb, 0, None) for t in range(subs) for qb in range(2) for h in range(heads)]

    def shared_blocks(jj, carry):
        state, first_max = carry
        state = [list(per_block) for per_block in state]
        first_max = run(state, shared(2 * jj) + shared(2 * jj + 1), first_max, following=shared(2 * jj + 2)[:2])
        return tuple(tuple(per_block) for per_block in state), first_max

    init = tuple(
        tuple((jnp.full((1, blk), NEG_BIG, jnp.float32), jnp.zeros((dh + ONES_ROWS, blk), jnp.float32))
              for _ in range(heads)) for _ in range(2))
    assert len(shared(0)) % slots == 0
    first_max = (produce(shared(0)[0], 0), produce(shared(0)[1], 1))
    state, first_max = lax.fori_loop(0, i, shared_blocks, (init, first_max))

    ja, jb = 2 * i, 2 * i + 1
    every = range(heads)
    tail = (
        [(ja, 0, h, 0, 0, "consume" if h < 2 else "produce") for h in every]
        + [(ja, 0, h, 1, 0, None) for h in every]
        + [(ja, 1, h, 0, half, "produce") for h in every]
        + [(ja, 1, h, 1, 0, None) for h in every]
        + [(jb, 0, h, 1, 0, "produce") for h in every]
        + [(jb, 1, h, 1, half, "produce") for h in every]
    )
    state = [list(per_block) for per_block in state]
    run(state, tail, first_max, after={3 * heads - 1: lambda: finish(state, 0)})
    finish(state, 1)


def _fox_stage(fq_t, fk, fv, fg):
    batch, seq, _ = fk.shape
    blk = FOX_BLOCK
    heads = FOX_STEP_HEADS
    width = heads * FOX_HEAD_DIM
    groups = FOX_HEADS // heads
    steps = seq // (2 * blk)
    tile = pl.BlockSpec((1, 2 * blk, width), lambda b, p, i: (b, i, p))
    return pl.pallas_call(
        _fox_kernel,
        grid=(batch, groups, steps),
        in_specs=[
            pl.BlockSpec((width, 2 * blk), lambda b, p, i: (p, b * steps + i)),
            pl.BlockSpec((1, seq, 2 * width), lambda b, p, i: (b, 0, p)),
            pl.BlockSpec((1, seq, width), lambda b, p, i: (b, 0, p)),
            tile,
        ],
        out_specs=tile,
        out_shape=jax.ShapeDtypeStruct((batch, seq, FOX_WIDTH), jnp.bfloat16),
        scratch_shapes=[
            pltpu.VMEM((heads, seq // FOX_KEYS, FOX_HEAD_DIM + ONES_ROWS, FOX_KEYS), jnp.bfloat16),
            pltpu.VMEM((4 * blk // FOX_KEYS, FOX_KEYS, blk), jnp.float32),
            pltpu.VMEM((FOX_KEYS, blk), jnp.float32),
        ],
        compiler_params=pltpu.CompilerParams(
            dimension_semantics=("arbitrary", "arbitrary", "arbitrary"), vmem_limit_bytes=VMEM_LIMIT),
        name="fox_stage",
    )(fq_t, fk, fv, fg)


def _ret_kernel(lg_ref, q_ref, k_ref, v_ref, g_ref, mf_ref, wo_ref, gain_ref, x_ref, o_ref,
                mixed_ref, state_ref, decay_ref, qdec_ref, kdec_ref, sdec_ref):
    blk = q_ref.shape[1]

    @pl.when((pl.program_id(0) == 0) & (pl.program_id(1) == 0))
    def _():
        row = lax.broadcasted_iota(jnp.int32, (blk, blk), 0)
        col = lax.broadcasted_iota(jnp.int32, (blk, blk), 1)
        dist = jnp.abs(row - col).astype(jnp.float32)
        pos = lax.broadcasted_iota(jnp.int32, (blk, LANES), 0).astype(jnp.float32)
        for h in range(RET_HEADS):
            lg = lg_ref[h]
            decay_ref[h] = jnp.where(col // CHUNK <= row // CHUNK, jnp.exp(lg * dist), 0.0)
            qdec_ref[h] = jnp.exp(lg * (pos + 1.0))
            kdec_ref[h] = jnp.exp(lg * (blk - 1.0 - pos))
            sdec_ref[h] = jnp.exp(lg * jnp.full((8, LANES), blk, jnp.float32))

    t = pl.program_id(1)

    @pl.when(t == 0)
    def _():
        state_ref[...] = jnp.zeros_like(state_ref)
        mixed_ref[...] = jnp.zeros_like(mixed_ref)

    previous = jnp.concatenate([mf_ref[0], mixed_ref[(t + 1) % 2]], axis=1)

    def project(part):
        return _dot(previous, wo_ref[:, part * (D_MODEL // 2):(part + 1) * (D_MODEL // 2)])

    def head_cols(h):
        return slice(h * LANES, (h + 1) * LANES)

    def first_matmuls(h):
        q = q_ref[0, :, head_cols(h)]
        return _dot_nt(q, k_ref[0, :, head_cols(h)]), _dot(q, state_ref[h].astype(jnp.bfloat16))

    def rest_of_head(h, scores, cross):
        cols = head_cols(h)
        k = k_ref[0, :, cols]
        v = v_ref[0, :, cols]
        inner = _dot((scores * decay_ref[h]).astype(jnp.bfloat16), v)
        kd = (k.astype(jnp.float32) * kdec_ref[h]).astype(jnp.bfloat16)
        state_ref[h] = state_ref[h] * sdec_ref[h, 0:1, :] + _dot_tn(kd, v)

        o = inner + cross * qdec_ref[h]
        mu = jnp.mean(o, axis=-1, keepdims=True)
        d = o - mu
        var = jnp.mean(d * d, axis=-1, keepdims=True)
        y = d * lax.rsqrt(var + GN_EPS)
        mixed_ref[t % 2, :, cols] = (y * g_ref[0, :, cols].astype(jnp.float32)).astype(mixed_ref.dtype)

    projected = []
    pending = first_matmuls(0)
    for h in range(RET_HEADS):
        if h + 1 < RET_HEADS:
            following = first_matmuls(h + 1)
        else:
            projected.append(project(0))
        rest_of_head(h, *pending)
        pending = following
    projected.append(project(1))

    out = jnp.concatenate(projected, axis=1)
    ms = jnp.mean(out * out, axis=-1, keepdims=True)
    o_ref[0] = x_ref[0] + out * lax.rsqrt(ms + NORM_EPS) * gain_ref[...]


def _ret_output_stage(log_gamma, rq, rk, rv, rg, mixed_fox, wo, gain, x):
    batch, seq, _ = rq.shape
    blk = RET_BLOCK
    steps = seq // blk
    tile = pl.BlockSpec((1, blk, RET_WIDTH), lambda b, t: (b, jnp.minimum(t, steps - 1), 0))
    lagged = pl.BlockSpec((1, blk, FOX_WIDTH), lambda b, t: (b, jnp.maximum(t - 1, 0), 0))
    wide = pl.BlockSpec((1, blk, D_MODEL), lambda b, t: (b, jnp.maximum(t - 1, 0), 0))
    return pl.pallas_call(
        _ret_kernel,
        grid=(batch, steps + 1),
        in_specs=[
            pl.BlockSpec(memory_space=pltpu.SMEM), tile, tile, tile, tile, lagged,
            pl.BlockSpec((FOX_WIDTH + RET_WIDTH, D_MODEL), lambda b, t: (0, 0)),
            pl.BlockSpec((1, D_MODEL), lambda b, t: (0, 0)),
            wide,
        ],
        out_specs=wide,
        out_shape=jax.ShapeDtypeStruct((batch, seq, D_MODEL), jnp.float32),
        scratch_shapes=[
            pltpu.VMEM((2, blk, RET_WIDTH), jnp.bfloat16),
            pltpu.VMEM((RET_HEADS, RET_HEAD_DIM, RET_HEAD_DIM), jnp.float32),
            pltpu.VMEM((RET_HEADS, blk, blk), jnp.float32),
            pltpu.VMEM((RET_HEADS, blk, LANES), jnp.float32),
            pltpu.VMEM((RET_HEADS, blk, LANES), jnp.float32),
            pltpu.VMEM((RET_HEADS, 8, LANES), jnp.float32),
        ],
        compiler_params=pltpu.CompilerParams(
            dimension_semantics=("arbitrary", "arbitrary"), vmem_limit_bytes=VMEM_LIMIT),
        name="ret_output_stage",
    )(log_gamma, rq, rk, rv, rg, mixed_fox, wo, gain, x)


def _layer(h, pre_gain, w_in, forget_bias, w_out, post_gain, cos_t, sin_t, log_gamma):
    batch, seq, _ = h.shape
    x2 = h.reshape(batch * seq, D_MODEL)
    wb = w_in.astype(jnp.bfloat16)
    fox_cols = wb[:, :4 * FOX_WIDTH].reshape(D_MODEL, 4, FOX_WIDTH)
    ret_cols = wb[:, 4 * FOX_WIDTH + FOX_HEADS:].reshape(D_MODEL, 4, RET_WIDTH)
    w8 = jnp.concatenate([fox_cols, ret_cols], axis=1).transpose(1, 0, 2)
    wf = jnp.pad(jnp.repeat(wb[:, 4 * FOX_WIDTH:4 * FOX_WIDTH + FOX_HEADS], 3, axis=1),
                 ((0, 0), (0, LANES - 3 * FOX_HEADS)))
    fbias = jnp.pad(jnp.repeat(forget_bias.astype(jnp.float32), 3), (0, LANES - 3 * FOX_HEADS))[None, :]

    fq, fk, fv, fg, rq, rk, rv, rg = _input_stage(
        x2, pre_gain[None, :], w8, wf, fbias, cos_t, sin_t, seq)

    def seq3(a):
        return a.reshape(batch, seq, a.shape[-1])

    mixed_fox = _fox_stage(fq, seq3(fk), seq3(fv), seq3(fg))
    return _ret_output_stage(log_gamma, seq3(rq), seq3(rk), seq3(rv), seq3(rg), mixed_fox,
                             w_out.astype(jnp.bfloat16), post_gain[None, :], h)


def kernel(x, pre_norm_gain, w_in, fox_forget_bias, w_out, post_norm_gain):
    seq = x.shape[1]
    cos_t, sin_t = _rope_tables(seq)
    log_gamma = jnp.log1p(
        -jnp.exp(jnp.linspace(math.log(1.0 / 32), math.log(1.0 / 512), RET_HEADS))).astype(jnp.float32)
    h = x
    for layer in range(pre_norm_gain.shape[0]):
        h = _layer(h, pre_norm_gain[layer], w_in[layer], fox_forget_bias[layer], w_out[layer],
                   post_norm_gain[layer], cos_t, sin_t, log_gamma)
    return h
```

```python
import functools
import math

import jax
import jax.numpy as jnp
from jax import lax
from jax.experimental import pallas as pl
from jax.experimental.pallas import tpu as pltpu

D_MODEL = 1024
CHUNK = 64
FOX_HEADS = 8
FOX_HEAD_DIM = 64
FOX_WIDTH = FOX_HEADS * FOX_HEAD_DIM
RET_HEADS = 4
RET_HEAD_DIM = 128
RET_WIDTH = RET_HEADS * RET_HEAD_DIM
ROPE_BASE = 10000.0
NORM_EPS = 1e-6
GN_EPS = 1e-5

LANES = 128
LOG2E = math.log2(math.e)
NEG_BIG = -1e30

IN_ROWS = 512
FOX_BLOCK = 512
FOX_KEYS = 256
FOX_STEP_HEADS = 4
RET_BLOCK = 256
VMEM_LIMIT = 48 * 1024 * 1024


def _dot(a, b):
    return jnp.dot(a, b, preferred_element_type=jnp.float32)


def _dot_nt(a, b):
    return lax.dot_general(a, b, (((1,), (1,)), ((), ())), preferred_element_type=jnp.float32)


def _dot_tn(a, b):
    return lax.dot_general(a, b, (((0,), (0,)), ((), ())), preferred_element_type=jnp.float32)


def _rope_table_kernel(inv_ref, cos_ref, sin_ref):
    rows = cos_ref.shape[0]
    pos = (lax.broadcasted_iota(jnp.int32, (rows, LANES), 0) + pl.program_id(0) * rows).astype(jnp.float32)
    lane = lax.broadcasted_iota(jnp.int32, (rows, LANES), 1)
    ang = pos * inv_ref[...]
    cos_ref[...] = jnp.cos(ang)
    sin_ref[...] = jnp.where(lane < LANES // 2, -1.0, 1.0) * jnp.sin(ang)


def _rope_tables(seq):
    half = RET_HEAD_DIM // 2
    inv = 1.0 / (ROPE_BASE ** (jnp.arange(half, dtype=jnp.float32) / half))
    inv2 = jnp.concatenate([inv, inv])[None, :]
    rows = 512
    return pl.pallas_call(
        _rope_table_kernel,
        grid=(seq // rows,),
        in_specs=[pl.BlockSpec((1, LANES), lambda i: (0, 0))],
        out_specs=[pl.BlockSpec((rows, LANES), lambda i: (i, 0))] * 2,
        out_shape=[jax.ShapeDtypeStruct((seq, LANES), jnp.float32)] * 2,
        name="rope_tables",
    )(inv2)


def _input_kernel(x_ref, g_ref, w_ref, wf_ref, fb_ref, cos_ref, sin_ref,
                  fq_ref, fk_ref, fv_ref, fg_ref, r_ref,
                  carry_ref, z_ref, *, seq_tiles):
    rows = x_ref.shape[0]

    @pl.when(pl.program_id(0) % seq_tiles == 0)
    def _():
        carry_ref[...] = jnp.zeros_like(carry_ref)

    x = x_ref[...]
    ms = jnp.mean(x * x, axis=-1, keepdims=True)
    u = (x * lax.rsqrt(ms + NORM_EPS) * g_ref[...]).astype(jnp.bfloat16)

    def silu(z):
        return z * (1.0 / (1.0 + jnp.exp(-z)))

    def rotary(z):
        cos = cos_ref[...]
        sin = sin_ref[...]
        heads = []
        for h in range(RET_HEADS):
            zh = z[:, h * LANES:(h + 1) * LANES]
            heads.append(zh * cos + pltpu.roll(zh, LANES // 2, axis=1) * sin)
        return jnp.concatenate(heads, axis=1)

    def store_fk(z):
        fk = z.astype(jnp.bfloat16)
        for p in range(FOX_HEADS // 2):
            fk_ref[:, 2 * p * LANES:(2 * p + 1) * LANES] = fk[:, p * LANES:(p + 1) * LANES]

    def store(ref, fn):
        def epilogue(z):
            ref[...] = fn(z).astype(jnp.bfloat16)
        return epilogue

    def ret_part(part):
        return r_ref.at[:, part * RET_WIDTH:(part + 1) * RET_WIDTH]

    stages = [
        (3, store(fg_ref, silu)),
        (4, store(ret_part(0), rotary)),
        (5, store(ret_part(1), lambda z: rotary(z) * (RET_HEAD_DIM ** -0.5))),
        (0, store(fq_ref, lambda z: jnp.transpose(z * (FOX_HEAD_DIM ** -0.5 * LOG2E)))),
        (1, store_fk),
        (2, store(fv_ref, lambda z: z)),
        (6, store(ret_part(2), lambda z: z)),
    ]

    used = 3 * FOX_HEADS
    f = jnp.transpose(_dot(u, wf_ref[...]) + fb_ref[...])[:used, :]
    chain = {"y": -(jnp.maximum(-f, 0.0) + jnp.log1p(jnp.exp(-jnp.abs(f)))) * LOG2E}
    pos = lax.broadcasted_iota(jnp.int32, (used, rows), 1)
    shifts = [1 << b for b in range(rows.bit_length() - 1)]

    def scan_steps(some):
        def piece():
            y = chain["y"]
            for shift in some:
                y = y + jnp.where(pos >= shift, pltpu.roll(y, shift, axis=1), 0.0)
            chain["y"] = y
        return piece

    def add_carry():
        y = chain["y"] + carry_ref[:, 0:1]
        carry_ref[...] = jnp.broadcast_to(y[:, rows - 1:rows], carry_ref.shape)
        chain["y"] = y

    def split_pieces():
        neg_c = -chain["y"]
        hi = neg_c.astype(jnp.bfloat16).astype(jnp.float32)
        rest = neg_c - hi
        mid = rest.astype(jnp.bfloat16).astype(jnp.float32)
        lo = rest - mid
        copy = lax.broadcasted_iota(jnp.int32, neg_c.shape, 0) % 3
        piece = jnp.where(copy == 0, hi, jnp.where(copy == 1, mid, lo))
        padded = jnp.concatenate([piece, jnp.zeros((LANES - used, rows), jnp.float32)], axis=0)
        chain["piece"] = jnp.transpose(padded)

    def store_pieces():
        piece = chain["piece"]
        lane = lax.broadcasted_iota(jnp.int32, piece.shape, 1)
        pair_lanes = 2 * 3
        for p in range(FOX_HEADS // 2):
            shifted = piece if p == 0 else pltpu.roll(piece, LANES - pair_lanes * p, axis=1)
            extra = jnp.where(lane < pair_lanes, shifted, 0.0).astype(jnp.bfloat16)
            fk_ref[:, (2 * p + 1) * LANES:(2 * p + 2) * LANES] = extra

    chain_pieces = [scan_steps(shifts[0:3]), scan_steps(shifts[3:6]), scan_steps(shifts[6:]), add_carry,
                    split_pieces, store_pieces]
    assert len(chain_pieces) == len(stages) - 1

    for n, (segment, _) in enumerate(stages):
        z_ref[n % 2] = _dot(u, w_ref[segment])
        if n > 0:
            stages[n - 1][1](z_ref[(n - 1) % 2])
            chain_pieces[n - 1]()
    stages[-1][1](z_ref[(len(stages) - 1) % 2])


def _input_stage(x2, gain, w7, wf, fbias, cos_t, sin_t, seq):
    n = x2.shape[0]
    rows = IN_ROWS
    seq_tiles = seq // rows
    act = jax.ShapeDtypeStruct((n, FOX_WIDTH), jnp.bfloat16)
    act_spec = pl.BlockSpec((rows, FOX_WIDTH), lambda i: (i, 0))
    wide = jax.ShapeDtypeStruct((n, 2 * FOX_WIDTH), jnp.bfloat16)
    wide_spec = pl.BlockSpec((rows, 2 * FOX_WIDTH), lambda i: (i, 0))
    tab_spec = pl.BlockSpec((rows, LANES), lambda i: (i % seq_tiles, 0))
    return pl.pallas_call(
        functools.partial(_input_kernel, seq_tiles=seq_tiles),
        grid=(n // rows,),
        in_specs=[
            pl.BlockSpec((rows, D_MODEL), lambda i: (i, 0)),
            pl.BlockSpec((1, D_MODEL), lambda i: (0, 0)),
            pl.BlockSpec((7, D_MODEL, FOX_WIDTH), lambda i: (0, 0, 0)),
            pl.BlockSpec((D_MODEL, LANES), lambda i: (0, 0)),
            pl.BlockSpec((1, LANES), lambda i: (0, 0)),
            tab_spec,
            tab_spec,
        ],
        out_specs=[pl.BlockSpec((FOX_WIDTH, rows), lambda i: (0, i)), wide_spec, act_spec, act_spec,
                   pl.BlockSpec((rows, 3 * RET_WIDTH), lambda i: (i, 0))],
        out_shape=[jax.ShapeDtypeStruct((FOX_WIDTH, n), jnp.bfloat16), wide, act, act,
                   jax.ShapeDtypeStruct((n, 3 * RET_WIDTH), jnp.bfloat16)],
        scratch_shapes=[
            pltpu.VMEM((3 * FOX_HEADS, LANES), jnp.float32),
            pltpu.VMEM((2, rows, FOX_WIDTH), jnp.float32),
        ],
        compiler_params=pltpu.CompilerParams(dimension_semantics=("arbitrary",), vmem_limit_bytes=VMEM_LIMIT),
        name="input_stage",
    )(x2, gain, w7, wf, fbias, cos_t, sin_t)


ONES_ROWS = 16


def _fox_kernel(q_ref, k_ref, v_ref, g_ref, o_ref, vt_ref, s_ref, bias_ref):
    blk = g_ref.shape[1] // 2
    i = pl.program_id(2)
    dh = FOX_HEAD_DIM
    heads = vt_ref.shape[0]
    sub = vt_ref.shape[3]
    subs = blk // sub
    half = blk // 2
    slots = s_ref.shape[0]

    @pl.when(i == 0)
    def _():
        key = lax.broadcasted_iota(jnp.int32, bias_ref.shape, 0)
        query = lax.broadcasted_iota(jnp.int32, bias_ref.shape, 1)
        bias_ref[...] = jnp.where(key <= query, 0.0, NEG_BIG)
        ones = jnp.ones((ONES_ROWS, sub), jnp.bfloat16)
        for jb in range(v_ref.shape[1] // sub):
            vt = jnp.transpose(v_ref[0, jb * sub:(jb + 1) * sub, :].astype(jnp.float32))
            for h in range(heads):
                vt_ref[h, jb, 0:dh, :] = vt[h * dh:(h + 1) * dh].astype(jnp.bfloat16)
                vt_ref[h, jb, dh:dh + ONES_ROWS, :] = ones

    def query_operands(qb):
        per_head = []
        for h in range(heads):
            pair, second = divmod(h, 2)
            qt = q_ref[pair * LANES:(pair + 1) * LANES, qb * blk:(qb + 1) * blk].astype(jnp.float32)
            row = lax.broadcasted_iota(jnp.int32, qt.shape, 0)
            own = jnp.where((row >= second * dh) & (row < (second + 1) * dh), qt, 0.0)
            ones = jnp.where((row >= 3 * second) & (row < 3 * second + 3), 1.0, 0.0)
            per_head.append(jnp.concatenate([own, ones], axis=0).astype(jnp.bfloat16))
        return per_head

    qts = [query_operands(0), query_operands(1)]

    def causal(s, t, lo):
        assert t * sub == lo
        return s + bias_ref[:, 0:s.shape[1]]

    def produce(unit, slot):
        j, t, h, qb, lo, mask = unit
        start = pl.multiple_of(j * blk + t * sub, sub)
        k_cols = slice(h // 2 * 2 * LANES, (h // 2 + 1) * 2 * LANES)
        s = _dot(k_ref[0, pl.ds(start, sub), k_cols], qts[qb][h][:, lo:])
        if mask == "produce":
            s = causal(s, t, lo)
        s_ref[slot, :, lo:] = s
        return jnp.max(s, axis=0, keepdims=True)

    def consume(state, unit, slot, col_max):
        j, t, h, qb, lo, mask = unit
        m, acc = state[qb][h]
        s = s_ref[slot, :, lo:]
        if mask == "consume":
            s = causal(s, t, lo)
            col_max = jnp.max(s, axis=0, keepdims=True)
        m_new = jnp.maximum(m[:, lo:], col_max)
        p = jnp.exp2(s - m_new).astype(jnp.bfloat16)
        acc_new = jnp.exp2(m[:, lo:] - m_new) * acc[:, lo:] + _dot(vt_ref[h, j * subs + t], p)
        if lo:
            m_new = jnp.concatenate([m[:, :lo], m_new], axis=1)
            acc_new = jnp.concatenate([acc[:, :lo], acc_new], axis=1)
        state[qb][h] = (m_new, acc_new)

    def finish(state, qb):
        out_t = jnp.concatenate([acc[0:dh] / acc[dh:dh + 1] for _, acc in state[qb]], axis=0)
        rows = slice(qb * blk, (qb + 1) * blk)
        o_ref[0, rows, :] = (jnp.transpose(out_t) * g_ref[0, rows, :].astype(jnp.float32)).astype(o_ref.dtype)

    def run(state, units, first_max, following=(), after=None):
        pending = dict(enumerate(first_max))
        stream = list(units) + list(following)
        for n, unit in enumerate(units):
            if n + 2 < len(stream):
                pending[n + 2] = produce(stream[n + 2], (n + 2) % slots)
            consume(state, unit, n % slots, pending.pop(n))
            if after and n in after:
                after[n]()
        return tuple(pending[n] for n in sorted(pending))

    def shared(j):
        return [(j, t, h, qb, 0, None) for t in range(subs) for qb in range(2) for h in range(heads)]

    def shared_blocks(jj, carry):
        state, first_max = carry
        state = [list(per_block) for per_block in state]
        first_max = run(state, shared(2 * jj) + shared(2 * jj + 1), first_max, following=shared(2 * jj + 2)[:2])
        return tuple(tuple(per_block) for per_block in state), first_max

    init = tuple(
        tuple((jnp.full((1, blk), NEG_BIG, jnp.float32), jnp.zeros((dh + ONES_ROWS, blk), jnp.float32))
              for _ in range(heads)) for _ in range(2))
    assert len(shared(0)) % slots == 0
    first_max = (produce(shared(0)[0], 0), produce(shared(0)[1], 1))
    state, first_max = lax.fori_loop(0, i, shared_blocks, (init, first_max))

    ja, jb = 2 * i, 2 * i + 1
    every = range(heads)
    tail = (
        [(ja, 0, h, 0, 0, "consume" if h < 2 else "produce") for h in every]
        + [(ja, 0, h, 1, 0, None) for h in every]
        + [(ja, 1, h, 0, half, "produce") for h in every]
        + [(ja, 1, h, 1, 0, None) for h in every]
        + [(jb, 0, h, 1, 0, "produce") for h in every]
        + [(jb, 1, h, 1, half, "produce") for h in every]
    )
    state = [list(per_block) for per_block in state]
    run(state, tail, first_max, after={3 * heads - 1: lambda: finish(state, 0)})
    finish(state, 1)


def _fox_stage(fq_t, fk, fv, fg):
    batch, seq, _ = fk.shape
    blk = FOX_BLOCK
    heads = FOX_STEP_HEADS
    width = heads * FOX_HEAD_DIM
    groups = FOX_HEADS // heads
    steps = seq // (2 * blk)
    tile = pl.BlockSpec((1, 2 * blk, width), lambda b, p, i: (b, i, p))
    return pl.pallas_call(
        _fox_kernel,
        grid=(batch, groups, steps),
        in_specs=[
            pl.BlockSpec((width, 2 * blk), lambda b, p, i: (p, b * steps + i)),
            pl.BlockSpec((1, seq, 2 * width), lambda b, p, i: (b, 0, p)),
            pl.BlockSpec((1, seq, width), lambda b, p, i: (b, 0, p)),
            tile,
        ],
        out_specs=tile,
        out_shape=jax.ShapeDtypeStruct((batch, seq, FOX_WIDTH), jnp.bfloat16),
        scratch_shapes=[
            pltpu.VMEM((heads, seq // FOX_KEYS, FOX_HEAD_DIM + ONES_ROWS, FOX_KEYS), jnp.bfloat16),
            pltpu.VMEM((4 * blk // FOX_KEYS, FOX_KEYS, blk), jnp.float32),
            pltpu.VMEM((FOX_KEYS, blk), jnp.float32),
        ],
        compiler_params=pltpu.CompilerParams(
            dimension_semantics=("arbitrary", "arbitrary", "arbitrary"), vmem_limit_bytes=VMEM_LIMIT),
        name="fox_stage",
    )(fq_t, fk, fv, fg)


def _ret_kernel(lg_ref, r_ref, mf_ref, wo_ref, gain_ref, x_ref, pre_gain_ref, wg_ref, o_ref,
                mixed_ref, x_prev_ref, state_ref, decay_ref, qdec_ref, kdec_ref, sdec_ref):
    blk = r_ref.shape[1]

    @pl.when((pl.program_id(0) == 0) & (pl.program_id(1) == 0))
    def _():
        row = lax.broadcasted_iota(jnp.int32, (blk, blk), 0)
        col = lax.broadcasted_iota(jnp.int32, (blk, blk), 1)
        dist = jnp.abs(row - col).astype(jnp.float32)
        pos = lax.broadcasted_iota(jnp.int32, (blk, LANES), 0).astype(jnp.float32)
        for h in range(RET_HEADS):
            lg = lg_ref[h]
            decay_ref[h] = jnp.where(col // CHUNK <= row // CHUNK, jnp.exp(lg * dist), 0.0)
            qdec_ref[h] = jnp.exp(lg * (pos + 1.0))
            kdec_ref[h] = jnp.exp(lg * (blk - 1.0 - pos))
            sdec_ref[h] = jnp.exp(lg * jnp.full((8, LANES), blk, jnp.float32))

    t = pl.program_id(1)

    @pl.when(t == 0)
    def _():
        state_ref[...] = jnp.zeros_like(state_ref)
        mixed_ref[...] = jnp.zeros_like(mixed_ref)
        x_prev_ref[...] = jnp.zeros_like(x_prev_ref)

    previous = jnp.concatenate([mf_ref[0], mixed_ref[(t + 1) % 2]], axis=1)

    def project(part):
        return _dot(previous, wo_ref[:, part * (D_MODEL // 2):(part + 1) * (D_MODEL // 2)])

    def head_cols(h, part=0):
        first = part * RET_WIDTH + h * LANES
        return slice(first, first + LANES)

    def first_matmuls(h):
        q = r_ref[0, :, head_cols(h, 0)]
        return _dot_nt(q, r_ref[0, :, head_cols(h, 1)]), _dot(q, state_ref[h].astype(jnp.bfloat16))

    def rest_of_head(h, scores, cross):
        cols = head_cols(h)
        k = r_ref[0, :, head_cols(h, 1)]
        v = r_ref[0, :, head_cols(h, 2)]
        inner = _dot((scores * decay_ref[h]).astype(jnp.bfloat16), v)
        kd = (k.astype(jnp.float32) * kdec_ref[h]).astype(jnp.bfloat16)
        state_ref[h] = state_ref[h] * sdec_ref[h, 0:1, :] + _dot_tn(kd, v)

        o = inner + cross * qdec_ref[h]
        mu = jnp.mean(o, axis=-1, keepdims=True)
        d = o - mu
        var = jnp.mean(d * d, axis=-1, keepdims=True)
        y = d * lax.rsqrt(var + GN_EPS)
        mixed_ref[t % 2, :, cols] = (y * gate_all[:, cols]).astype(mixed_ref.dtype)

    projected = [project(0)]
    pending = first_matmuls(0)
    x = x_ref[0]
    u = (x * lax.rsqrt(jnp.mean(x * x, axis=-1, keepdims=True) + NORM_EPS) * pre_gain_ref[...]).astype(jnp.bfloat16)
    z = _dot(u, wg_ref[...])
    gate_all = z * (1.0 / (1.0 + jnp.exp(-z)))
    for h in range(RET_HEADS):
        if h + 1 < RET_HEADS:
            following = first_matmuls(h + 1)
        else:
            projected.append(project(1))
        rest_of_head(h, *pending)
        pending = following

    out = jnp.concatenate(projected, axis=1)
    ms = jnp.mean(out * out, axis=-1, keepdims=True)
    o_ref[0] = x_prev_ref[...] + out * lax.rsqrt(ms + NORM_EPS) * gain_ref[...]
    x_prev_ref[...] = x_ref[0]


def _ret_output_stage(log_gamma, ret_in, mixed_fox, wo, gain, x, pre_gain, w_gate):
    batch, seq, _ = ret_in.shape
    blk = RET_BLOCK
    steps = seq // blk
    current = lambda b, t: (b, jnp.minimum(t, steps - 1), 0)
    lagging = lambda b, t: (b, jnp.maximum(t - 1, 0), 0)
    return pl.pallas_call(
        _ret_kernel,
        grid=(batch, steps + 1),
        in_specs=[
            pl.BlockSpec(memory_space=pltpu.SMEM),
            pl.BlockSpec((1, blk, 3 * RET_WIDTH), current),
            pl.BlockSpec((1, blk, FOX_WIDTH), lagging),
            pl.BlockSpec((FOX_WIDTH + RET_WIDTH, D_MODEL), lambda b, t: (0, 0)),
            pl.BlockSpec((1, D_MODEL), lambda b, t: (0, 0)),
            pl.BlockSpec((1, blk, D_MODEL), current),
            pl.BlockSpec((1, D_MODEL), lambda b, t: (0, 0)),
            pl.BlockSpec((D_MODEL, RET_WIDTH), lambda b, t: (0, 0)),
        ],
        out_specs=pl.BlockSpec((1, blk, D_MODEL), lagging),
        out_shape=jax.ShapeDtypeStruct((batch, seq, D_MODEL), jnp.float32),
        scratch_shapes=[
            pltpu.VMEM((2, blk, RET_WIDTH), jnp.bfloat16),
            pltpu.VMEM((blk, D_MODEL), jnp.float32),
            pltpu.VMEM((RET_HEADS, RET_HEAD_DIM, RET_HEAD_DIM), jnp.float32),
            pltpu.VMEM((RET_HEADS, blk, blk), jnp.float32),
            pltpu.VMEM((RET_HEADS, blk, LANES), jnp.float32),
            pltpu.VMEM((RET_HEADS, blk, LANES), jnp.float32),
            pltpu.VMEM((RET_HEADS, 8, LANES), jnp.float32),
        ],
        compiler_params=pltpu.CompilerParams(
            dimension_semantics=("arbitrary", "arbitrary"), vmem_limit_bytes=VMEM_LIMIT),
        name="ret_output_stage",
    )(log_gamma, ret_in, mixed_fox, wo, gain, x, pre_gain, w_gate)


def _layer(h, pre_gain, w_in, forget_bias, w_out, post_gain, cos_t, sin_t, log_gamma):
    batch, seq, _ = h.shape
    x2 = h.reshape(batch * seq, D_MODEL)
    wb = w_in.astype(jnp.bfloat16)
    fox_cols = wb[:, :4 * FOX_WIDTH].reshape(D_MODEL, 4, FOX_WIDTH)
    ret_cols = wb[:, 4 * FOX_WIDTH + FOX_HEADS:].reshape(D_MODEL, 4, RET_WIDTH)
    w8 = jnp.concatenate([fox_cols, ret_cols], axis=1).transpose(1, 0, 2)
    wf = jnp.pad(jnp.repeat(wb[:, 4 * FOX_WIDTH:4 * FOX_WIDTH + FOX_HEADS], 3, axis=1),
                 ((0, 0), (0, LANES - 3 * FOX_HEADS)))
    fbias = jnp.pad(jnp.repeat(forget_bias.astype(jnp.float32), 3), (0, LANES - 3 * FOX_HEADS))[None, :]

    fq, fk, fv, fg, ret_in = _input_stage(
        x2, pre_gain[None, :], w8[:7], wf, fbias, cos_t, sin_t, seq)

    def seq3(a):
        return a.reshape(batch, seq, a.shape[-1])

    mixed_fox = _fox_stage(fq, seq3(fk), seq3(fv), seq3(fg))
    return _ret_output_stage(log_gamma, seq3(ret_in), mixed_fox, w_out.astype(jnp.bfloat16),
                             post_gain[None, :], h, pre_gain[None, :], w8[7])


def kernel(x, pre_norm_gain, w_in, fox_forget_bias, w_out, post_norm_gain):
    seq = x.shape[1]
    cos_t, sin_t = _rope_tables(seq)
    log_gamma = jnp.log1p(
        -jnp.exp(jnp.linspace(math.log(1.0 / 32), math.log(1.0 / 512), RET_HEADS))).astype(jnp.float32)
    h = x
    for layer in range(pre_norm_gain.shape[0]):
        h = _layer(h, pre_norm_gain[layer], w_in[layer], fox_forget_bias[layer], w_out[layer],
                   post_norm_gain[layer], cos_t, sin_t, log_gamma)
    return h
```

```python
import functools
import math

import jax
import jax.numpy as jnp
from jax import lax
from jax.experimental import pallas as pl
from jax.experimental.pallas import tpu as pltpu

D_MODEL = 1024
CHUNK = 64
FOX_HEADS = 8
FOX_HEAD_DIM = 64
FOX_WIDTH = FOX_HEADS * FOX_HEAD_DIM
RET_HEADS = 4
RET_HEAD_DIM = 128
RET_WIDTH = RET_HEADS * RET_HEAD_DIM
ROPE_BASE = 10000.0
NORM_EPS = 1e-6
GN_EPS = 1e-5

LANES = 128
LOG2E = math.log2(math.e)
NEG_BIG = -1e30

IN_ROWS = 512
FOX_BLOCK = 512
FOX_KEYS = 256
FOX_STEP_HEADS = 4
RET_BLOCK = 256
VMEM_LIMIT = 48 * 1024 * 1024


def _dot(a, b):
    return jnp.dot(a, b, preferred_element_type=jnp.float32)


def _dot_nt(a, b):
    return lax.dot_general(a, b, (((1,), (1,)), ((), ())), preferred_element_type=jnp.float32)


def _dot_tn(a, b):
    return lax.dot_general(a, b, (((0,), (0,)), ((), ())), preferred_element_type=jnp.float32)


def _rope_table_kernel(inv_ref, cos_ref, sin_ref):
    rows = cos_ref.shape[0]
    pos = (lax.broadcasted_iota(jnp.int32, (rows, LANES), 0) + pl.program_id(0) * rows).astype(jnp.float32)
    lane = lax.broadcasted_iota(jnp.int32, (rows, LANES), 1)
    ang = pos * inv_ref[...]
    cos_ref[...] = jnp.cos(ang)
    sin_ref[...] = jnp.where(lane < LANES // 2, -1.0, 1.0) * jnp.sin(ang)


def _rope_tables(seq):
    half = RET_HEAD_DIM // 2
    inv = 1.0 / (ROPE_BASE ** (jnp.arange(half, dtype=jnp.float32) / half))
    inv2 = jnp.concatenate([inv, inv])[None, :]
    rows = 512
    return pl.pallas_call(
        _rope_table_kernel,
        grid=(seq // rows,),
        in_specs=[pl.BlockSpec((1, LANES), lambda i: (0, 0))],
        out_specs=[pl.BlockSpec((rows, LANES), lambda i: (i, 0))] * 2,
        out_shape=[jax.ShapeDtypeStruct((seq, LANES), jnp.float32)] * 2,
        name="rope_tables",
    )(inv2)


def _input_kernel(x_ref, g_ref, wfox_ref, wret_ref, wf_ref, fb_ref, cos_ref, sin_ref,
                  fq_ref, fk_ref, fv_ref, fg_ref, r_ref,
                  carry_ref, z_ref, *, seq_tiles):
    rows = x_ref.shape[0]

    @pl.when(pl.program_id(0) % seq_tiles == 0)
    def _():
        carry_ref[...] = jnp.zeros_like(carry_ref)

    x = x_ref[...]
    ms = jnp.mean(x * x, axis=-1, keepdims=True)
    u = (x * lax.rsqrt(ms + NORM_EPS) * g_ref[...]).astype(jnp.bfloat16)

    def silu(z):
        return z * (1.0 / (1.0 + jnp.exp(-z)))

    def rotary(z):
        cos = cos_ref[...]
        sin = sin_ref[...]
        heads = []
        for h in range(RET_HEADS):
            zh = z[:, h * LANES:(h + 1) * LANES]
            heads.append(zh * cos + pltpu.roll(zh, LANES // 2, axis=1) * sin)
        return jnp.concatenate(heads, axis=1)

    def store_fk(z):
        fk = z.astype(jnp.bfloat16)
        for p in range(FOX_HEADS // 2):
            fk_ref[:, 2 * p * LANES:(2 * p + 1) * LANES] = fk[:, p * LANES:(p + 1) * LANES]

    def store(ref, fn):
        def epilogue(z):
            ref[...] = fn(z).astype(jnp.bfloat16)
        return epilogue

    def ret_part(part):
        return r_ref.at[:, part * RET_WIDTH:(part + 1) * RET_WIDTH]

    stages = [
        (3, store(fg_ref, silu)),
        (4, store(ret_part(0), rotary)),
        (5, store(ret_part(1), lambda z: rotary(z) * (RET_HEAD_DIM ** -0.5))),
        (0, store(fq_ref, lambda z: jnp.transpose(z * (FOX_HEAD_DIM ** -0.5 * LOG2E)))),
        (1, store_fk),
        (2, store(fv_ref, lambda z: z)),
        (6, store(ret_part(2), lambda z: z)),
    ]

    used = 3 * FOX_HEADS
    f = jnp.transpose(_dot(u, wf_ref[...]) + fb_ref[...])[:used, :]
    chain = {"y": -(jnp.maximum(-f, 0.0) + jnp.log1p(jnp.exp(-jnp.abs(f)))) * LOG2E}
    pos = lax.broadcasted_iota(jnp.int32, (used, rows), 1)
    shifts = [1 << b for b in range(rows.bit_length() - 1)]

    def scan_steps(some):
        def piece():
            y = chain["y"]
            for shift in some:
                y = y + jnp.where(pos >= shift, pltpu.roll(y, shift, axis=1), 0.0)
            chain["y"] = y
        return piece

    def add_carry():
        y = chain["y"] + carry_ref[:, 0:1]
        carry_ref[...] = jnp.broadcast_to(y[:, rows - 1:rows], carry_ref.shape)
        chain["y"] = y

    def split_pieces():
        neg_c = -chain["y"]
        hi = neg_c.astype(jnp.bfloat16).astype(jnp.float32)
        rest = neg_c - hi
        mid = rest.astype(jnp.bfloat16).astype(jnp.float32)
        lo = rest - mid
        copy = lax.broadcasted_iota(jnp.int32, neg_c.shape, 0) % 3
        piece = jnp.where(copy == 0, hi, jnp.where(copy == 1, mid, lo))
        padded = jnp.concatenate([piece, jnp.zeros((LANES - used, rows), jnp.float32)], axis=0)
        chain["piece"] = jnp.transpose(padded)

    def store_pieces():
        piece = chain["piece"]
        lane = lax.broadcasted_iota(jnp.int32, piece.shape, 1)
        pair_lanes = 2 * 3
        for p in range(FOX_HEADS // 2):
            shifted = piece if p == 0 else pltpu.roll(piece, LANES - pair_lanes * p, axis=1)
            extra = jnp.where(lane < pair_lanes, shifted, 0.0).astype(jnp.bfloat16)
            fk_ref[:, (2 * p + 1) * LANES:(2 * p + 2) * LANES] = extra

    chain_pieces = [scan_steps(shifts[0:3]), scan_steps(shifts[3:6]), scan_steps(shifts[6:]), add_carry,
                    split_pieces, store_pieces]
    assert len(chain_pieces) == len(stages) - 1

    for n, (segment, _) in enumerate(stages):
        group_ref, part = (wfox_ref, segment) if segment < 4 else (wret_ref, segment - 4)
        z_ref[n % 2] = _dot(u, group_ref[:, part * FOX_WIDTH:(part + 1) * FOX_WIDTH])
        if n > 0:
            stages[n - 1][1](z_ref[(n - 1) % 2])
            chain_pieces[n - 1]()
    stages[-1][1](z_ref[(len(stages) - 1) % 2])


def _input_stage(x2, gain, w_fox, w_ret, wf, fbias, cos_t, sin_t, seq):
    n = x2.shape[0]
    rows = IN_ROWS
    seq_tiles = seq // rows
    act = jax.ShapeDtypeStruct((n, FOX_WIDTH), jnp.bfloat16)
    act_spec = pl.BlockSpec((rows, FOX_WIDTH), lambda i: (i, 0))
    wide = jax.ShapeDtypeStruct((n, 2 * FOX_WIDTH), jnp.bfloat16)
    wide_spec = pl.BlockSpec((rows, 2 * FOX_WIDTH), lambda i: (i, 0))
    tab_spec = pl.BlockSpec((rows, LANES), lambda i: (i % seq_tiles, 0))
    return pl.pallas_call(
        functools.partial(_input_kernel, seq_tiles=seq_tiles),
        grid=(n // rows,),
        in_specs=[
            pl.BlockSpec((rows, D_MODEL), lambda i: (i, 0)),
            pl.BlockSpec((1, D_MODEL), lambda i: (0, 0)),
            pl.BlockSpec((D_MODEL, 4 * FOX_WIDTH), lambda i: (0, 0)),
            pl.BlockSpec((D_MODEL, 4 * RET_WIDTH), lambda i: (0, 0)),
            pl.BlockSpec((D_MODEL, LANES), lambda i: (0, 0)),
            pl.BlockSpec((1, LANES), lambda i: (0, 0)),
            tab_spec,
            tab_spec,
        ],
        out_specs=[pl.BlockSpec((FOX_WIDTH, rows), lambda i: (0, i)), wide_spec, act_spec, act_spec,
                   pl.BlockSpec((rows, 3 * RET_WIDTH), lambda i: (i, 0))],
        out_shape=[jax.ShapeDtypeStruct((FOX_WIDTH, n), jnp.bfloat16), wide, act, act,
                   jax.ShapeDtypeStruct((n, 3 * RET_WIDTH), jnp.bfloat16)],
        scratch_shapes=[
            pltpu.VMEM((3 * FOX_HEADS, LANES), jnp.float32),
            pltpu.VMEM((2, rows, FOX_WIDTH), jnp.float32),
        ],
        compiler_params=pltpu.CompilerParams(dimension_semantics=("arbitrary",), vmem_limit_bytes=VMEM_LIMIT),
        name="input_stage",
    )(x2, gain, w_fox, w_ret, wf, fbias, cos_t, sin_t)


ONES_ROWS = 16


def _fox_kernel(q_ref, k_ref, v_ref, g_ref, o_ref, vt_ref, s_ref, bias_ref):
    blk = g_ref.shape[1] // 2
    i = pl.program_id(2)
    dh = FOX_HEAD_DIM
    heads = vt_ref.shape[0]
    sub = vt_ref.shape[3]
    subs = blk // sub
    half = blk // 2
    ahead = 2
    slots = s_ref.shape[0]

    @pl.when(i == 0)
    def _():
        key = lax.broadcasted_iota(jnp.int32, bias_ref.shape, 0)
        query = lax.broadcasted_iota(jnp.int32, bias_ref.shape, 1)
        bias_ref[...] = jnp.where(key <= query, 0.0, NEG_BIG)
        ones = jnp.ones((ONES_ROWS, sub), jnp.bfloat16)
        for jb in range(v_ref.shape[1] // sub):
            vt = jnp.transpose(v_ref[0, jb * sub:(jb + 1) * sub, :].astype(jnp.float32))
            for h in range(heads):
                vt_ref[h, jb, 0:dh, :] = vt[h * dh:(h + 1) * dh].astype(jnp.bfloat16)
                vt_ref[h, jb, dh:dh + ONES_ROWS, :] = ones

    def query_operands(qb):
        per_head = []
        for h in range(heads):
            pair, second = divmod(h, 2)
            qt = q_ref[pair * LANES:(pair + 1) * LANES, qb * blk:(qb + 1) * blk].astype(jnp.float32)
            row = lax.broadcasted_iota(jnp.int32, qt.shape, 0)
            own = jnp.where((row >= second * dh) & (row < (second + 1) * dh), qt, 0.0)
            ones = jnp.where((row >= 3 * second) & (row < 3 * second + 3), 1.0, 0.0)
            per_head.append(jnp.concatenate([own, ones], axis=0).astype(jnp.bfloat16))
        return per_head

    qts = [query_operands(0), query_operands(1)]

    def causal(s, t, lo):
        assert t * sub == lo
        return s + bias_ref[:, 0:s.shape[1]]

    def produce(unit, slot):
        j, t, h, qb, lo, mask = unit
        start = pl.multiple_of(j * blk + t * sub, sub)
        k_cols = slice(h // 2 * 2 * LANES, (h // 2 + 1) * 2 * LANES)
        s = _dot(k_ref[0, pl.ds(start, sub), k_cols], qts[qb][h][:, lo:])
        if mask == "produce":
            s = causal(s, t, lo)
        s_ref[slot, :, lo:] = s
        return jnp.max(s, axis=0, keepdims=True)

    def consume(state, unit, slot, col_max):
        j, t, h, qb, lo, mask = unit
        m, acc = state[qb][h]
        s = s_ref[slot, :, lo:]
        if mask == "consume":
            s = causal(s, t, lo)
            col_max = jnp.max(s, axis=0, keepdims=True)
        m_new = jnp.maximum(m[:, lo:], col_max)
        p = jnp.exp2(s - m_new).astype(jnp.bfloat16)
        acc_new = jnp.exp2(m[:, lo:] - m_new) * acc[:, lo:] + _dot(vt_ref[h, j * subs + t], p)
        if lo:
            m_new = jnp.concatenate([m[:, :lo], m_new], axis=1)
            acc_new = jnp.concatenate([acc[:, :lo], acc_new], axis=1)
        state[qb][h] = (m_new, acc_new)

    def finish(state, qb):
        out_t = jnp.concatenate([acc[0:dh] / acc[dh:dh + 1] for _, acc in state[qb]], axis=0)
        rows = slice(qb * blk, (qb + 1) * blk)
        o_ref[0, rows, :] = (jnp.transpose(out_t) * g_ref[0, rows, :].astype(jnp.float32)).astype(o_ref.dtype)

    def run(state, units, first_max, following=(), after=None):
        pending = dict(enumerate(first_max))
        stream = list(units) + list(following)
        for n, unit in enumerate(units):
            if n + ahead < len(stream):
                pending[n + ahead] = produce(stream[n + ahead], (n + ahead) % slots)
            consume(state, unit, n % slots, pending.pop(n))
            if after and n in after:
                after[n]()
        return tuple(pending[n] for n in sorted(pending))

    def shared(j):
        return [(j, t, h, qb, 0, None) for t in range(subs) for qb in range(2) for h in range(heads)]

    def shared_blocks(jj, carry):
        state, first_max = carry
        state = [list(per_block) for per_block in state]
        first_max = run(state, shared(2 * jj) + shared(2 * jj + 1), first_max, following=shared(2 * jj + 2)[:ahead])
        return tuple(tuple(per_block) for per_block in state), first_max

    init = tuple(
        tuple((jnp.full((1, blk), NEG_BIG, jnp.float32), jnp.zeros((dh + ONES_ROWS, blk), jnp.float32))
              for _ in range(heads)) for _ in range(2))
    assert len(shared(0)) % slots == 0
    first_max = tuple(produce(shared(0)[n], n) for n in range(ahead))
    state, first_max = lax.fori_loop(0, i, shared_blocks, (init, first_max))

    ja, jb = 2 * i, 2 * i + 1
    every = range(heads)
    tail = (
        [(ja, 0, h, 0, 0, "consume" if h < ahead else "produce") for h in every]
        + [(ja, 0, h, 1, 0, None) for h in every]
        + [(ja, 1, h, 0, half, "produce") for h in every]
        + [(ja, 1, h, 1, 0, None) for h in every]
        + [(jb, 0, h, 1, 0, "produce") for h in every]
        + [(jb, 1, h, 1, half, "produce") for h in every]
    )
    state = [list(per_block) for per_block in state]
    run(state, tail, first_max, after={3 * heads - 1: lambda: finish(state, 0)})
    finish(state, 1)


def _fox_stage(fq_t, fk, fv, fg):
    batch, seq, _ = fk.shape
    blk = FOX_BLOCK
    heads = FOX_STEP_HEADS
    width = heads * FOX_HEAD_DIM
    groups = FOX_HEADS // heads
    steps = seq // (2 * blk)
    tile = pl.BlockSpec((1, 2 * blk, width), lambda b, p, i: (b, i, p))
    return pl.pallas_call(
        _fox_kernel,
        grid=(batch, groups, steps),
        in_specs=[
            pl.BlockSpec((width, 2 * blk), lambda b, p, i: (p, b * steps + i)),
            pl.BlockSpec((1, seq, 2 * width), lambda b, p, i: (b, 0, p)),
            pl.BlockSpec((1, seq, width), lambda b, p, i: (b, 0, p)),
            tile,
        ],
        out_specs=tile,
        out_shape=jax.ShapeDtypeStruct((batch, seq, FOX_WIDTH), jnp.bfloat16),
        scratch_shapes=[
            pltpu.VMEM((heads, seq // FOX_KEYS, FOX_HEAD_DIM + ONES_ROWS, FOX_KEYS), jnp.bfloat16),
            pltpu.VMEM((4 * blk // FOX_KEYS, FOX_KEYS, blk), jnp.float32),
            pltpu.VMEM((FOX_KEYS, blk), jnp.float32),
        ],
        compiler_params=pltpu.CompilerParams(
            dimension_semantics=("arbitrary", "arbitrary", "arbitrary"), vmem_limit_bytes=VMEM_LIMIT),
        name="fox_stage",
    )(fq_t, fk, fv, fg)


def _ret_kernel(lg_ref, r_ref, mf_ref, wo_ref, gain_ref, x_ref, pre_gain_ref, wg_ref, o_ref,
                mixed_ref, x_prev_ref, state_ref, decay_ref, qdec_ref, kdec_ref, sdec_ref):
    blk = r_ref.shape[1]

    @pl.when((pl.program_id(0) == 0) & (pl.program_id(1) == 0))
    def _():
        row = lax.broadcasted_iota(jnp.int32, (blk, blk), 0)
        col = lax.broadcasted_iota(jnp.int32, (blk, blk), 1)
        dist = jnp.abs(row - col).astype(jnp.float32)
        pos = lax.broadcasted_iota(jnp.int32, (blk, LANES), 0).astype(jnp.float32)
        for h in range(RET_HEADS):
            lg = lg_ref[h]
            decay_ref[h] = jnp.where(col // CHUNK <= row // CHUNK, jnp.exp(lg * dist), 0.0)
            qdec_ref[h] = jnp.exp(lg * (pos + 1.0))
            kdec_ref[h] = jnp.exp(lg * (blk - 1.0 - pos))
            sdec_ref[h] = jnp.exp(lg * jnp.full((8, LANES), blk, jnp.float32))

    t = pl.program_id(1)

    @pl.when(t == 0)
    def _():
        state_ref[...] = jnp.zeros_like(state_ref)
        mixed_ref[...] = jnp.zeros_like(mixed_ref)
        x_prev_ref[...] = jnp.zeros_like(x_prev_ref)

    previous = jnp.concatenate([mf_ref[0], mixed_ref[(t + 1) % 2]], axis=1)

    def project(part):
        return _dot(previous, wo_ref[:, part * (D_MODEL // 2):(part + 1) * (D_MODEL // 2)])

    def head_cols(h, part=0):
        first = part * RET_WIDTH + h * LANES
        return slice(first, first + LANES)

    def first_matmuls(h):
        q = r_ref[0, :, head_cols(h, 0)]
        return _dot_nt(q, r_ref[0, :, head_cols(h, 1)]), _dot(q, state_ref[h].astype(jnp.bfloat16))

    def rest_of_head(h, scores, cross):
        cols = head_cols(h)
        k = r_ref[0, :, head_cols(h, 1)]
        v = r_ref[0, :, head_cols(h, 2)]
        inner = _dot((scores * decay_ref[h]).astype(jnp.bfloat16), v)
        kd = (k.astype(jnp.float32) * kdec_ref[h]).astype(jnp.bfloat16)
        state_ref[h] = state_ref[h] * sdec_ref[h, 0:1, :] + _dot_tn(kd, v)

        o = inner + cross * qdec_ref[h]
        mu = jnp.mean(o, axis=-1, keepdims=True)
        d = o - mu
        var = jnp.mean(d * d, axis=-1, keepdims=True)
        y = d * lax.rsqrt(var + GN_EPS)
        mixed_ref[t % 2, :, cols] = (y * gate_all[:, cols]).astype(mixed_ref.dtype)

    projected = [project(0)]
    pending = first_matmuls(0)
    x = x_ref[0]
    u = (x * lax.rsqrt(jnp.mean(x * x, axis=-1, keepdims=True) + NORM_EPS) * pre_gain_ref[...]).astype(jnp.bfloat16)
    z = _dot(u, wg_ref[...])
    gate_all = z * (1.0 / (1.0 + jnp.exp(-z)))
    for h in range(RET_HEADS):
        if h + 1 < RET_HEADS:
            following = first_matmuls(h + 1)
        else:
            projected.append(project(1))
        rest_of_head(h, *pending)
        pending = following

    out = jnp.concatenate(projected, axis=1)
    ms = jnp.mean(out * out, axis=-1, keepdims=True)
    o_ref[0] = x_prev_ref[...] + out * lax.rsqrt(ms + NORM_EPS) * gain_ref[...]
    x_prev_ref[...] = x_ref[0]


def _ret_output_stage(log_gamma, ret_in, mixed_fox, wo, gain, x, pre_gain, w_ret):
    batch, seq, _ = ret_in.shape
    blk = RET_BLOCK
    steps = seq // blk
    current = lambda b, t: (b, jnp.minimum(t, steps - 1), 0)
    lagging = lambda b, t: (b, jnp.maximum(t - 1, 0), 0)
    return pl.pallas_call(
        _ret_kernel,
        grid=(batch, steps + 1),
        in_specs=[
            pl.BlockSpec(memory_space=pltpu.SMEM),
            pl.BlockSpec((1, blk, 3 * RET_WIDTH), current),
            pl.BlockSpec((1, blk, FOX_WIDTH), lagging),
            pl.BlockSpec((FOX_WIDTH + RET_WIDTH, D_MODEL), lambda b, t: (0, 0)),
            pl.BlockSpec((1, D_MODEL), lambda b, t: (0, 0)),
            pl.BlockSpec((1, blk, D_MODEL), current),
            pl.BlockSpec((1, D_MODEL), lambda b, t: (0, 0)),
            pl.BlockSpec((D_MODEL, RET_WIDTH), lambda b, t: (0, 3)),
        ],
        out_specs=pl.BlockSpec((1, blk, D_MODEL), lagging),
        out_shape=jax.ShapeDtypeStruct((batch, seq, D_MODEL), jnp.float32),
        scratch_shapes=[
            pltpu.VMEM((2, blk, RET_WIDTH), jnp.bfloat16),
            pltpu.VMEM((blk, D_MODEL), jnp.float32),
            pltpu.VMEM((RET_HEADS, RET_HEAD_DIM, RET_HEAD_DIM), jnp.float32),
            pltpu.VMEM((RET_HEADS, blk, blk), jnp.float32),
            pltpu.VMEM((RET_HEADS, blk, LANES), jnp.float32),
            pltpu.VMEM((RET_HEADS, blk, LANES), jnp.float32),
            pltpu.VMEM((RET_HEADS, 8, LANES), jnp.float32),
        ],
        compiler_params=pltpu.CompilerParams(
            dimension_semantics=("arbitrary", "arbitrary"), vmem_limit_bytes=VMEM_LIMIT),
        name="ret_output_stage",
    )(log_gamma, ret_in, mixed_fox, wo, gain, x, pre_gain, w_ret)


def _layer(h, pre_gain, w_in, forget_bias, w_out, post_gain, cos_t, sin_t, log_gamma):
    batch, seq, _ = h.shape
    x2 = h.reshape(batch * seq, D_MODEL)
    w_fox = w_in[:, :4 * FOX_WIDTH].astype(jnp.bfloat16)
    w_ret = w_in[:, 4 * FOX_WIDTH + FOX_HEADS:].astype(jnp.bfloat16)
    wf = jnp.pad(jnp.repeat(w_in[:, 4 * FOX_WIDTH:4 * FOX_WIDTH + FOX_HEADS].astype(jnp.bfloat16), 3, axis=1),
                 ((0, 0), (0, LANES - 3 * FOX_HEADS)))
    fbias = jnp.pad(jnp.repeat(forget_bias.astype(jnp.float32), 3), (0, LANES - 3 * FOX_HEADS))[None, :]

    fq, fk, fv, fg, ret_in = _input_stage(
        x2, pre_gain[None, :], w_fox, w_ret, wf, fbias, cos_t, sin_t, seq)

    def seq3(a):
        return a.reshape(batch, seq, a.shape[-1])

    mixed_fox = _fox_stage(fq, seq3(fk), seq3(fv), seq3(fg))
    return _ret_output_stage(log_gamma, seq3(ret_in), mixed_fox, w_out.astype(jnp.bfloat16),
                             post_gain[None, :], h, pre_gain[None, :], w_ret)


def kernel(x, pre_norm_gain, w_in, fox_forget_bias, w_out, post_norm_gain):
    seq = x.shape[1]
    cos_t, sin_t = _rope_tables(seq)
    log_gamma = jnp.log1p(
        -jnp.exp(jnp.linspace(math.log(1.0 / 32), math.log(1.0 / 512), RET_HEADS))).astype(jnp.float32)
    h = x
    for layer in range(pre_norm_gain.shape[0]):
        h = _layer(h, pre_norm_gain[layer], w_in[layer], fox_forget_bias[layer], w_out[layer],
                   post_norm_gain[layer], cos_t, sin_t, log_gamma)
    return h
```

```python
import functools
import math

import jax
import jax.numpy as jnp
from jax import lax
from jax.experimental import pallas as pl
from jax.experimental.pallas import tpu as pltpu

D_MODEL = 1024
CHUNK = 64
FOX_HEADS = 8
FOX_HEAD_DIM = 64
FOX_WIDTH = FOX_HEADS * FOX_HEAD_DIM
RET_HEADS = 4
RET_HEAD_DIM = 128
RET_WIDTH = RET_HEADS * RET_HEAD_DIM
ROPE_BASE = 10000.0
NORM_EPS = 1e-6
GN_EPS = 1e-5

LANES = 128
LOG2E = math.log2(math.e)
NEG_BIG = -1e30

IN_ROWS = 512
FOX_BLOCK = 512
FOX_KEYS = 256
FOX_STEP_HEADS = 4
RET_BLOCK = 256
VMEM_LIMIT = 48 * 1024 * 1024


def _dot(a, b):
    return jnp.dot(a, b, preferred_element_type=jnp.float32)


def _dot_nt(a, b):
    return lax.dot_general(a, b, (((1,), (1,)), ((), ())), preferred_element_type=jnp.float32)


def _dot_tn(a, b):
    return lax.dot_general(a, b, (((0,), (0,)), ((), ())), preferred_element_type=jnp.float32)


def _rope_table_kernel(inv_ref, cos_ref, sin_ref):
    rows = cos_ref.shape[0]
    pos = (lax.broadcasted_iota(jnp.int32, (rows, LANES), 0) + pl.program_id(0) * rows).astype(jnp.float32)
    lane = lax.broadcasted_iota(jnp.int32, (rows, LANES), 1)
    ang = pos * inv_ref[...]
    cos_ref[...] = jnp.cos(ang)
    sin_ref[...] = jnp.where(lane < LANES // 2, -1.0, 1.0) * jnp.sin(ang)


def _rope_tables(seq):
    half = RET_HEAD_DIM // 2
    inv = 1.0 / (ROPE_BASE ** (jnp.arange(half, dtype=jnp.float32) / half))
    inv2 = jnp.concatenate([inv, inv])[None, :]
    rows = 512
    return pl.pallas_call(
        _rope_table_kernel,
        grid=(seq // rows,),
        in_specs=[pl.BlockSpec((1, LANES), lambda i: (0, 0))],
        out_specs=[pl.BlockSpec((rows, LANES), lambda i: (i, 0))] * 2,
        out_shape=[jax.ShapeDtypeStruct((seq, LANES), jnp.float32)] * 2,
        name="rope_tables",
    )(inv2)


def _input_kernel(x_ref, g_ref, wfox_ref, wret_ref, wf_ref, fb_ref, cos_ref, sin_ref,
                  fq_ref, fk_ref, fv_ref, fg_ref, r_ref,
                  carry_ref, z_ref, *, seq_tiles):
    rows = x_ref.shape[0]

    @pl.when(pl.program_id(0) % seq_tiles == 0)
    def _():
        carry_ref[...] = jnp.zeros_like(carry_ref)

    x = x_ref[...]
    ms = jnp.mean(x * x, axis=-1, keepdims=True)
    u = (x * lax.rsqrt(ms + NORM_EPS) * g_ref[...]).astype(jnp.bfloat16)

    def silu(z):
        return z * (1.0 / (1.0 + jnp.exp(-z)))

    def rotary(z):
        cos = cos_ref[...]
        sin = sin_ref[...]
        heads = []
        for h in range(RET_HEADS):
            zh = z[:, h * LANES:(h + 1) * LANES]
            heads.append(zh * cos + pltpu.roll(zh, LANES // 2, axis=1) * sin)
        return jnp.concatenate(heads, axis=1)

    def store_fk(z):
        fk = z.astype(jnp.bfloat16)
        for p in range(FOX_HEADS // 2):
            fk_ref[:, 2 * p * LANES:(2 * p + 1) * LANES] = fk[:, p * LANES:(p + 1) * LANES]

    def store(ref, fn):
        def epilogue(z):
            ref[...] = fn(z).astype(jnp.bfloat16)
        return epilogue

    def ret_part(part):
        return r_ref.at[:, part * RET_WIDTH:(part + 1) * RET_WIDTH]

    stages = [
        (3, store(fg_ref, silu)),
        (4, store(ret_part(0), rotary)),
        (5, store(ret_part(1), lambda z: rotary(z) * (RET_HEAD_DIM ** -0.5))),
        (0, store(fq_ref, lambda z: jnp.transpose(z * (FOX_HEAD_DIM ** -0.5 * LOG2E)))),
        (1, store_fk),
        (2, store(fv_ref, jnp.transpose)),
        (6, store(ret_part(2), lambda z: z)),
    ]

    used = 3 * FOX_HEADS
    f = jnp.transpose(_dot(u, wf_ref[...]) + fb_ref[...])[:used, :]
    chain = {"y": -(jnp.maximum(-f, 0.0) + jnp.log1p(jnp.exp(-jnp.abs(f)))) * LOG2E}
    pos = lax.broadcasted_iota(jnp.int32, (used, rows), 1)
    shifts = [1 << b for b in range(rows.bit_length() - 1)]

    def scan_steps(some):
        def piece():
            y = chain["y"]
            for shift in some:
                y = y + jnp.where(pos >= shift, pltpu.roll(y, shift, axis=1), 0.0)
            chain["y"] = y
        return piece

    def add_carry():
        y = chain["y"] + carry_ref[:, 0:1]
        carry_ref[...] = jnp.broadcast_to(y[:, rows - 1:rows], carry_ref.shape)
        chain["y"] = y

    def split_pieces():
        neg_c = -chain["y"]
        hi = neg_c.astype(jnp.bfloat16).astype(jnp.float32)
        rest = neg_c - hi
        mid = rest.astype(jnp.bfloat16).astype(jnp.float32)
        lo = rest - mid
        copy = lax.broadcasted_iota(jnp.int32, neg_c.shape, 0) % 3
        piece = jnp.where(copy == 0, hi, jnp.where(copy == 1, mid, lo))
        padded = jnp.concatenate([piece, jnp.zeros((LANES - used, rows), jnp.float32)], axis=0)
        chain["piece"] = jnp.transpose(padded)

    def store_pieces():
        piece = chain["piece"]
        lane = lax.broadcasted_iota(jnp.int32, piece.shape, 1)
        pair_lanes = 2 * 3
        for p in range(FOX_HEADS // 2):
            shifted = piece if p == 0 else pltpu.roll(piece, LANES - pair_lanes * p, axis=1)
            extra = jnp.where(lane < pair_lanes, shifted, 0.0).astype(jnp.bfloat16)
            fk_ref[:, (2 * p + 1) * LANES:(2 * p + 2) * LANES] = extra

    chain_pieces = [scan_steps(shifts[0:3]), scan_steps(shifts[3:6]), scan_steps(shifts[6:]), add_carry,
                    split_pieces, store_pieces]
    assert len(chain_pieces) == len(stages) - 1

    for n, (segment, _) in enumerate(stages):
        group_ref, part = (wfox_ref, segment) if segment < 4 else (wret_ref, segment - 4)
        z_ref[n % 2] = _dot(u, group_ref[:, part * FOX_WIDTH:(part + 1) * FOX_WIDTH])
        if n > 0:
            stages[n - 1][1](z_ref[(n - 1) % 2])
            chain_pieces[n - 1]()
    stages[-1][1](z_ref[(len(stages) - 1) % 2])


def _input_stage(x2, gain, w_fox, w_ret, wf, fbias, cos_t, sin_t, seq):
    n = x2.shape[0]
    rows = IN_ROWS
    seq_tiles = seq // rows
    act = jax.ShapeDtypeStruct((n, FOX_WIDTH), jnp.bfloat16)
    act_spec = pl.BlockSpec((rows, FOX_WIDTH), lambda i: (i, 0))
    wide = jax.ShapeDtypeStruct((n, 2 * FOX_WIDTH), jnp.bfloat16)
    wide_spec = pl.BlockSpec((rows, 2 * FOX_WIDTH), lambda i: (i, 0))
    transposed = jax.ShapeDtypeStruct((FOX_WIDTH, n), jnp.bfloat16)
    transposed_spec = pl.BlockSpec((FOX_WIDTH, rows), lambda i: (0, i))
    tab_spec = pl.BlockSpec((rows, LANES), lambda i: (i % seq_tiles, 0))
    return pl.pallas_call(
        functools.partial(_input_kernel, seq_tiles=seq_tiles),
        grid=(n // rows,),
        in_specs=[
            pl.BlockSpec((rows, D_MODEL), lambda i: (i, 0)),
            pl.BlockSpec((1, D_MODEL), lambda i: (0, 0)),
            pl.BlockSpec((D_MODEL, 4 * FOX_WIDTH), lambda i: (0, 0)),
            pl.BlockSpec((D_MODEL, 4 * RET_WIDTH), lambda i: (0, 0)),
            pl.BlockSpec((D_MODEL, LANES), lambda i: (0, 0)),
            pl.BlockSpec((1, LANES), lambda i: (0, 0)),
            tab_spec,
            tab_spec,
        ],
        out_specs=[transposed_spec, wide_spec, transposed_spec, act_spec,
                   pl.BlockSpec((rows, 3 * RET_WIDTH), lambda i: (i, 0))],
        out_shape=[transposed, wide, transposed, act,
                   jax.ShapeDtypeStruct((n, 3 * RET_WIDTH), jnp.bfloat16)],
        scratch_shapes=[
            pltpu.VMEM((3 * FOX_HEADS, LANES), jnp.float32),
            pltpu.VMEM((2, rows, FOX_WIDTH), jnp.float32),
        ],
        compiler_params=pltpu.CompilerParams(dimension_semantics=("arbitrary",), vmem_limit_bytes=VMEM_LIMIT),
        name="input_stage",
    )(x2, gain, w_fox, w_ret, wf, fbias, cos_t, sin_t)


ONES_ROWS = 16


def _fox_kernel(q_ref, k_ref, v_ref, g_ref, o_ref, vt_ref, s_ref, bias_ref):
    blk = g_ref.shape[1] // 2
    i = pl.program_id(2)
    dh = FOX_HEAD_DIM
    heads = vt_ref.shape[0]
    sub = vt_ref.shape[3]
    subs = blk // sub
    half = blk // 2
    ahead = 2
    slots = s_ref.shape[0]

    @pl.when(i == 0)
    def _():
        key = lax.broadcasted_iota(jnp.int32, bias_ref.shape, 0)
        query = lax.broadcasted_iota(jnp.int32, bias_ref.shape, 1)
        bias_ref[...] = jnp.where(key <= query, 0.0, NEG_BIG)
        ones = jnp.ones((ONES_ROWS, sub), jnp.bfloat16)
        for jb in range(v_ref.shape[1] // sub):
            for h in range(heads):
                vt_ref[h, jb, 0:dh, :] = v_ref[h * dh:(h + 1) * dh, jb * sub:(jb + 1) * sub]
                vt_ref[h, jb, dh:dh + ONES_ROWS, :] = ones

    def query_operands(qb):
        per_head = []
        for h in range(heads):
            pair, second = divmod(h, 2)
            qt = q_ref[pair * LANES:(pair + 1) * LANES, qb * blk:(qb + 1) * blk].astype(jnp.float32)
            row = lax.broadcasted_iota(jnp.int32, qt.shape, 0)
            own = jnp.where((row >= second * dh) & (row < (second + 1) * dh), qt, 0.0)
            ones = jnp.where((row >= 3 * second) & (row < 3 * second + 3), 1.0, 0.0)
            per_head.append(jnp.concatenate([own, ones], axis=0).astype(jnp.bfloat16))
        return per_head

    qts = [query_operands(0), query_operands(1)]

    def causal(s, t, lo):
        assert t * sub == lo
        return s + bias_ref[:, 0:s.shape[1]]

    def produce(unit, slot):
        j, t, h, qb, lo, mask = unit
        start = pl.multiple_of(j * blk + t * sub, sub)
        k_cols = slice(h // 2 * 2 * LANES, (h // 2 + 1) * 2 * LANES)
        s = _dot(k_ref[0, pl.ds(start, sub), k_cols], qts[qb][h][:, lo:])
        if mask == "produce":
            s = causal(s, t, lo)
        s_ref[slot, :, lo:] = s
        return jnp.max(s, axis=0, keepdims=True)

    def consume(state, unit, slot, col_max):
        j, t, h, qb, lo, mask = unit
        m, acc = state[qb][h]
        s = s_ref[slot, :, lo:]
        if mask == "consume":
            s = causal(s, t, lo)
            col_max = jnp.max(s, axis=0, keepdims=True)
        m_new = jnp.maximum(m[:, lo:], col_max)
        p = jnp.exp2(s - m_new).astype(jnp.bfloat16)
        acc_new = jnp.exp2(m[:, lo:] - m_new) * acc[:, lo:] + _dot(vt_ref[h, j * subs + t], p)
        if lo:
            m_new = jnp.concatenate([m[:, :lo], m_new], axis=1)
            acc_new = jnp.concatenate([acc[:, :lo], acc_new], axis=1)
        state[qb][h] = (m_new, acc_new)

    def finish(state, qb):
        out_t = jnp.concatenate([acc[0:dh] / acc[dh:dh + 1] for _, acc in state[qb]], axis=0)
        rows = slice(qb * blk, (qb + 1) * blk)
        o_ref[0, rows, :] = (jnp.transpose(out_t) * g_ref[0, rows, :].astype(jnp.float32)).astype(o_ref.dtype)

    def run(state, units, first_max, following=(), after=None):
        pending = dict(enumerate(first_max))
        stream = list(units) + list(following)
        for n, unit in enumerate(units):
            if n + ahead < len(stream):
                pending[n + ahead] = produce(stream[n + ahead], (n + ahead) % slots)
            consume(state, unit, n % slots, pending.pop(n))
            if after and n in after:
                after[n]()
        return tuple(pending[n] for n in sorted(pending))

    def shared(j):
        return [(j, t, h, qb, 0, None) for t in range(subs) for qb in range(2) for h in range(heads)]

    def shared_blocks(jj, carry):
        state, first_max = carry
        state = [list(per_block) for per_block in state]
        first_max = run(state, shared(2 * jj) + shared(2 * jj + 1), first_max, following=shared(2 * jj + 2)[:ahead])
        return tuple(tuple(per_block) for per_block in state), first_max

    init = tuple(
        tuple((jnp.full((1, blk), NEG_BIG, jnp.float32), jnp.zeros((dh + ONES_ROWS, blk), jnp.float32))
              for _ in range(heads)) for _ in range(2))
    assert len(shared(0)) % slots == 0
    first_max = tuple(produce(shared(0)[n], n) for n in range(ahead))
    state, first_max = lax.fori_loop(0, i, shared_blocks, (init, first_max))

    ja, jb = 2 * i, 2 * i + 1
    every = range(heads)
    tail = (
        [(ja, 0, h, 0, 0, "consume" if h < ahead else "produce") for h in every]
        + [(ja, 0, h, 1, 0, None) for h in every]
        + [(ja, 1, h, 0, half, "produce") for h in every]
        + [(ja, 1, h, 1, 0, None) for h in every]
        + [(jb, 0, h, 1, 0, "produce") for h in every]
        + [(jb, 1, h, 1, half, "produce") for h in every]
    )
    state = [list(per_block) for per_block in state]
    run(state, tail, first_max, after={3 * heads - 1: lambda: finish(state, 0)})
    finish(state, 1)


def _fox_stage(fq_t, fk, fv_t, fg):
    batch, seq, _ = fk.shape
    blk = FOX_BLOCK
    heads = FOX_STEP_HEADS
    width = heads * FOX_HEAD_DIM
    groups = FOX_HEADS // heads
    steps = seq // (2 * blk)
    tile = pl.BlockSpec((1, 2 * blk, width), lambda b, p, i: (b, i, p))
    return pl.pallas_call(
        _fox_kernel,
        grid=(batch, groups, steps),
        in_specs=[
            pl.BlockSpec((width, 2 * blk), lambda b, p, i: (p, b * steps + i)),
            pl.BlockSpec((1, seq, 2 * width), lambda b, p, i: (b, 0, p)),
            pl.BlockSpec((width, seq), lambda b, p, i: (p, b)),
            tile,
        ],
        out_specs=tile,
        out_shape=jax.ShapeDtypeStruct((batch, seq, FOX_WIDTH), jnp.bfloat16),
        scratch_shapes=[
            pltpu.VMEM((heads, seq // FOX_KEYS, FOX_HEAD_DIM + ONES_ROWS, FOX_KEYS), jnp.bfloat16),
            pltpu.VMEM((4 * blk // FOX_KEYS, FOX_KEYS, blk), jnp.float32),
            pltpu.VMEM((FOX_KEYS, blk), jnp.float32),
        ],
        compiler_params=pltpu.CompilerParams(
            dimension_semantics=("arbitrary", "arbitrary", "arbitrary"), vmem_limit_bytes=VMEM_LIMIT),
        name="fox_stage",
    )(fq_t, fk, fv_t, fg)


def _ret_kernel(lg_ref, r_ref, mf_ref, wo_ref, gain_ref, x_ref, pre_gain_ref, wg_ref, o_ref,
                mixed_ref, x_prev_ref, state_ref, decay_ref, qdec_ref, kdec_ref, sdec_ref):
    blk = r_ref.shape[1]

    @pl.when((pl.program_id(0) == 0) & (pl.program_id(1) == 0))
    def _():
        row = lax.broadcasted_iota(jnp.int32, (blk, blk), 0)
        col = lax.broadcasted_iota(jnp.int32, (blk, blk), 1)
        dist = jnp.abs(row - col).astype(jnp.float32)
        pos = lax.broadcasted_iota(jnp.int32, (blk, LANES), 0).astype(jnp.float32)
        for h in range(RET_HEADS):
            lg = lg_ref[h]
            decay_ref[h] = jnp.where(col // CHUNK <= row // CHUNK, jnp.exp(lg * dist), 0.0)
            qdec_ref[h] = jnp.exp(lg * (pos + 1.0))
            kdec_ref[h] = jnp.exp(lg * (blk - 1.0 - pos))
            sdec_ref[h] = jnp.exp(lg * jnp.full((8, LANES), blk, jnp.float32))

    t = pl.program_id(1)

    @pl.when(t == 0)
    def _():
        state_ref[...] = jnp.zeros_like(state_ref)
        mixed_ref[...] = jnp.zeros_like(mixed_ref)
        x_prev_ref[...] = jnp.zeros_like(x_prev_ref)

    previous = jnp.concatenate([mf_ref[0], mixed_ref[(t + 1) % 2]], axis=1)

    def project(part):
        return _dot(previous, wo_ref[:, part * (D_MODEL // 2):(part + 1) * (D_MODEL // 2)])

    def head_cols(h, part=0):
        first = part * RET_WIDTH + h * LANES
        return slice(first, first + LANES)

    def first_matmuls(h):
        q = r_ref[0, :, head_cols(h, 0)]
        return _dot_nt(q, r_ref[0, :, head_cols(h, 1)]), _dot(q, state_ref[h].astype(jnp.bfloat16))

    def rest_of_head(h, scores, cross):
        cols = head_cols(h)
        k = r_ref[0, :, head_cols(h, 1)]
        v = r_ref[0, :, head_cols(h, 2)]
        inner = _dot((scores * decay_ref[h]).astype(jnp.bfloat16), v)
        kd = (k.astype(jnp.float32) * kdec_ref[h]).astype(jnp.bfloat16)
        state_ref[h] = state_ref[h] * sdec_ref[h, 0:1, :] + _dot_tn(kd, v)

        o = inner + cross * qdec_ref[h]
        mu = jnp.mean(o, axis=-1, keepdims=True)
        d = o - mu
        var = jnp.mean(d * d, axis=-1, keepdims=True)
        y = d * lax.rsqrt(var + GN_EPS)
        mixed_ref[t % 2, :, cols] = (y * gate_all[:, cols]).astype(mixed_ref.dtype)

    projected = [project(0)]
    pending = first_matmuls(0)
    x = x_ref[0]
    u = (x * lax.rsqrt(jnp.mean(x * x, axis=-1, keepdims=True) + NORM_EPS) * pre_gain_ref[...]).astype(jnp.bfloat16)
    z = _dot(u, wg_ref[...])
    gate_all = z * (1.0 / (1.0 + jnp.exp(-z)))
    for h in range(RET_HEADS):
        if h + 1 < RET_HEADS:
            following = first_matmuls(h + 1)
        else:
            projected.append(project(1))
        rest_of_head(h, *pending)
        pending = following

    out = jnp.concatenate(projected, axis=1)
    ms = jnp.mean(out * out, axis=-1, keepdims=True)
    o_ref[0] = x_prev_ref[...] + out * lax.rsqrt(ms + NORM_EPS) * gain_ref[...]
    x_prev_ref[...] = x_ref[0]


def _ret_output_stage(log_gamma, ret_in, mixed_fox, wo, gain, x, pre_gain, w_ret):
    batch, seq, _ = ret_in.shape
    blk = RET_BLOCK
    steps = seq // blk
    current = lambda b, t: (b, jnp.minimum(t, steps - 1), 0)
    lagging = lambda b, t: (b, jnp.maximum(t - 1, 0), 0)
    return pl.pallas_call(
        _ret_kernel,
        grid=(batch, steps + 1),
        in_specs=[
            pl.BlockSpec(memory_space=pltpu.SMEM),
            pl.BlockSpec((1, blk, 3 * RET_WIDTH), current),
            pl.BlockSpec((1, blk, FOX_WIDTH), lagging),
            pl.BlockSpec((FOX_WIDTH + RET_WIDTH, D_MODEL), lambda b, t: (0, 0)),
            pl.BlockSpec((1, D_MODEL), lambda b, t: (0, 0)),
            pl.BlockSpec((1, blk, D_MODEL), current),
            pl.BlockSpec((1, D_MODEL), lambda b, t: (0, 0)),
            pl.BlockSpec((D_MODEL, RET_WIDTH), lambda b, t: (0, 3)),
        ],
        out_specs=pl.BlockSpec((1, blk, D_MODEL), lagging),
        out_shape=jax.ShapeDtypeStruct((batch, seq, D_MODEL), jnp.float32),
        scratch_shapes=[
            pltpu.VMEM((2, blk, RET_WIDTH), jnp.bfloat16),
            pltpu.VMEM((blk, D_MODEL), jnp.float32),
            pltpu.VMEM((RET_HEADS, RET_HEAD_DIM, RET_HEAD_DIM), jnp.float32),
            pltpu.VMEM((RET_HEADS, blk, blk), jnp.float32),
            pltpu.VMEM((RET_HEADS, blk, LANES), jnp.float32),
            pltpu.VMEM((RET_HEADS, blk, LANES), jnp.float32),
            pltpu.VMEM((RET_HEADS, 8, LANES), jnp.float32),
        ],
        compiler_params=pltpu.CompilerParams(
            dimension_semantics=("arbitrary", "arbitrary"), vmem_limit_bytes=VMEM_LIMIT),
        name="ret_output_stage",
    )(log_gamma, ret_in, mixed_fox, wo, gain, x, pre_gain, w_ret)


def _layer(h, pre_gain, w_in, forget_bias, w_out, post_gain, cos_t, sin_t, log_gamma):
    batch, seq, _ = h.shape
    x2 = h.reshape(batch * seq, D_MODEL)
    w_fox = w_in[:, :4 * FOX_WIDTH].astype(jnp.bfloat16)
    w_ret = w_in[:, 4 * FOX_WIDTH + FOX_HEADS:].astype(jnp.bfloat16)
    wf = jnp.pad(jnp.repeat(w_in[:, 4 * FOX_WIDTH:4 * FOX_WIDTH + FOX_HEADS].astype(jnp.bfloat16), 3, axis=1),
                 ((0, 0), (0, LANES - 3 * FOX_HEADS)))
    fbias = jnp.pad(jnp.repeat(forget_bias.astype(jnp.float32), 3), (0, LANES - 3 * FOX_HEADS))[None, :]

    fq, fk, fv, fg, ret_in = _input_stage(
        x2, pre_gain[None, :], w_fox, w_ret, wf, fbias, cos_t, sin_t, seq)

    def seq3(a):
        return a.reshape(batch, seq, a.shape[-1])

    mixed_fox = _fox_stage(fq, seq3(fk), fv, seq3(fg))
    return _ret_output_stage(log_gamma, seq3(ret_in), mixed_fox, w_out.astype(jnp.bfloat16),
                             post_gain[None, :], h, pre_gain[None, :], w_ret)


def kernel(x, pre_norm_gain, w_in, fox_forget_bias, w_out, post_norm_gain):
    seq = x.shape[1]
    cos_t, sin_t = _rope_tables(seq)
    log_gamma = jnp.log1p(
        -jnp.exp(jnp.linspace(math.log(1.0 / 32), math.log(1.0 / 512), RET_HEADS))).astype(jnp.float32)
    h = x
    for layer in range(pre_norm_gain.shape[0]):
        h = _layer(h, pre_norm_gain[layer], w_in[layer], fox_forget_bias[layer], w_out[layer],
                   post_norm_gain[layer], cos_t, sin_t, log_gamma)
    return h
```

```python
import functools
import math

import jax
import jax.numpy as jnp
from jax import lax
from jax.experimental import pallas as pl
from jax.experimental.pallas import tpu as pltpu

D_MODEL = 1024
CHUNK = 64
FOX_HEADS = 8
FOX_HEAD_DIM = 64
FOX_WIDTH = FOX_HEADS * FOX_HEAD_DIM
RET_HEADS = 4
RET_HEAD_DIM = 128
RET_WIDTH = RET_HEADS * RET_HEAD_DIM
ROPE_BASE = 10000.0
NORM_EPS = 1e-6
GN_EPS = 1e-5

LANES = 128
LOG2E = math.log2(math.e)
NEG_BIG = -1e30

IN_ROWS = 512
FOX_BLOCK = 512
FOX_KEYS = 256
FOX_STEP_HEADS = 8
RET_BLOCK = 256
VMEM_LIMIT = 54 * 1024 * 1024


def _dot(a, b):
    return jnp.dot(a, b, preferred_element_type=jnp.float32)


def _dot_nt(a, b):
    return lax.dot_general(a, b, (((1,), (1,)), ((), ())), preferred_element_type=jnp.float32)


def _dot_tn(a, b):
    return lax.dot_general(a, b, (((0,), (0,)), ((), ())), preferred_element_type=jnp.float32)


def _rope_table_kernel(inv_ref, cos_ref, sin_ref):
    rows = cos_ref.shape[0]
    pos = (lax.broadcasted_iota(jnp.int32, (rows, LANES), 0) + pl.program_id(0) * rows).astype(jnp.float32)
    lane = lax.broadcasted_iota(jnp.int32, (rows, LANES), 1)
    ang = pos * inv_ref[...]
    cos_ref[...] = jnp.cos(ang)
    sin_ref[...] = jnp.where(lane < LANES // 2, -1.0, 1.0) * jnp.sin(ang)


def _rope_tables(seq):
    half = RET_HEAD_DIM // 2
    inv = 1.0 / (ROPE_BASE ** (jnp.arange(half, dtype=jnp.float32) / half))
    inv2 = jnp.concatenate([inv, inv])[None, :]
    rows = 512
    return pl.pallas_call(
        _rope_table_kernel,
        grid=(seq // rows,),
        in_specs=[pl.BlockSpec((1, LANES), lambda i: (0, 0))],
        out_specs=[pl.BlockSpec((rows, LANES), lambda i: (i, 0))] * 2,
        out_shape=[jax.ShapeDtypeStruct((seq, LANES), jnp.float32)] * 2,
        name="rope_tables",
    )(inv2)


def _input_kernel(x_ref, g_ref, w_ref, wf_ref, fb_ref, cos_ref, sin_ref,
                  fq_ref, fk_ref, fv_ref, fg_ref, r_ref,
                  carry_ref, z_ref, *, seq_tiles):
    rows = x_ref.shape[0]

    @pl.when(pl.program_id(0) % seq_tiles == 0)
    def _():
        carry_ref[...] = jnp.zeros_like(carry_ref)

    x = x_ref[...]
    ms = jnp.mean(x * x, axis=-1, keepdims=True)
    u = (x * lax.rsqrt(ms + NORM_EPS) * g_ref[...]).astype(jnp.bfloat16)

    def silu(z):
        return z * (1.0 / (1.0 + jnp.exp(-z)))

    def rotary(z):
        cos = cos_ref[...]
        sin = sin_ref[...]
        heads = []
        for h in range(RET_HEADS):
            zh = z[:, h * LANES:(h + 1) * LANES]
            heads.append(zh * cos + pltpu.roll(zh, LANES // 2, axis=1) * sin)
        return jnp.concatenate(heads, axis=1)

    def store_fk(z):
        fk = z.astype(jnp.bfloat16)
        for p in range(FOX_HEADS // 2):
            fk_ref[:, 2 * p * LANES:(2 * p + 1) * LANES] = fk[:, p * LANES:(p + 1) * LANES]

    def store(ref, fn):
        def epilogue(z):
            ref[...] = fn(z).astype(jnp.bfloat16)
        return epilogue

    def ret_part(part):
        return r_ref.at[:, part * RET_WIDTH:(part + 1) * RET_WIDTH]

    stages = [
        (3, store(fg_ref, silu)),
        (4, store(ret_part(0), rotary)),
        (5, store(ret_part(1), lambda z: rotary(z) * (RET_HEAD_DIM ** -0.5))),
        (0, store(fq_ref, lambda z: jnp.transpose(z * (FOX_HEAD_DIM ** -0.5 * LOG2E)))),
        (1, store_fk),
        (2, store(fv_ref, lambda z: z)),
        (6, store(ret_part(2), lambda z: z)),
    ]

    used = 3 * FOX_HEADS
    f = jnp.transpose(_dot(u, wf_ref[...]) + fb_ref[...])[:used, :]
    chain = {"y": -(jnp.maximum(-f, 0.0) + jnp.log1p(jnp.exp(-jnp.abs(f)))) * LOG2E}
    pos = lax.broadcasted_iota(jnp.int32, (used, rows), 1)
    shifts = [1 << b for b in range(rows.bit_length() - 1)]

    def scan_steps(some):
        def piece():
            y = chain["y"]
            for shift in some:
                y = y + jnp.where(pos >= shift, pltpu.roll(y, shift, axis=1), 0.0)
            chain["y"] = y
        return piece

    def add_carry():
        y = chain["y"] + carry_ref[:, 0:1]
        carry_ref[...] = jnp.broadcast_to(y[:, rows - 1:rows], carry_ref.shape)
        chain["y"] = y

    def split_pieces():
        neg_c = -chain["y"]
        hi = neg_c.astype(jnp.bfloat16).astype(jnp.float32)
        rest = neg_c - hi
        mid = rest.astype(jnp.bfloat16).astype(jnp.float32)
        lo = rest - mid
        copy = lax.broadcasted_iota(jnp.int32, neg_c.shape, 0) % 3
        piece = jnp.where(copy == 0, hi, jnp.where(copy == 1, mid, lo))
        padded = jnp.concatenate([piece, jnp.zeros((LANES - used, rows), jnp.float32)], axis=0)
        chain["piece"] = jnp.transpose(padded)

    def store_pieces():
        piece = chain["piece"]
        lane = lax.broadcasted_iota(jnp.int32, piece.shape, 1)
        pair_lanes = 2 * 3
        for p in range(FOX_HEADS // 2):
            shifted = piece if p == 0 else pltpu.roll(piece, LANES - pair_lanes * p, axis=1)
            extra = jnp.where(lane < pair_lanes, shifted, 0.0).astype(jnp.bfloat16)
            fk_ref[:, (2 * p + 1) * LANES:(2 * p + 2) * LANES] = extra

    chain_pieces = [scan_steps(shifts[0:3]), scan_steps(shifts[3:6]), scan_steps(shifts[6:]), add_carry,
                    split_pieces, store_pieces]
    assert len(chain_pieces) == len(stages) - 1

    for n, (segment, _) in enumerate(stages):
        z_ref[n % 2] = _dot(u, w_ref[segment])
        if n > 0:
            stages[n - 1][1](z_ref[(n - 1) % 2])
            chain_pieces[n - 1]()
    stages[-1][1](z_ref[(len(stages) - 1) % 2])


def _input_stage(x2, gain, w7, wf, fbias, cos_t, sin_t, seq):
    n = x2.shape[0]
    rows = IN_ROWS
    seq_tiles = seq // rows
    act = jax.ShapeDtypeStruct((n, FOX_WIDTH), jnp.bfloat16)
    act_spec = pl.BlockSpec((rows, FOX_WIDTH), lambda i: (i, 0))
    wide = jax.ShapeDtypeStruct((n, 2 * FOX_WIDTH), jnp.bfloat16)
    wide_spec = pl.BlockSpec((rows, 2 * FOX_WIDTH), lambda i: (i, 0))
    tab_spec = pl.BlockSpec((rows, LANES), lambda i: (i % seq_tiles, 0))
    return pl.pallas_call(
        functools.partial(_input_kernel, seq_tiles=seq_tiles),
        grid=(n // rows,),
        in_specs=[
            pl.BlockSpec((rows, D_MODEL), lambda i: (i, 0)),
            pl.BlockSpec((1, D_MODEL), lambda i: (0, 0)),
            pl.BlockSpec((7, D_MODEL, FOX_WIDTH), lambda i: (0, 0, 0)),
            pl.BlockSpec((D_MODEL, LANES), lambda i: (0, 0)),
            pl.BlockSpec((1, LANES), lambda i: (0, 0)),
            tab_spec,
            tab_spec,
        ],
        out_specs=[pl.BlockSpec((FOX_WIDTH, rows), lambda i: (0, i)), wide_spec, act_spec, act_spec,
                   pl.BlockSpec((rows, 3 * RET_WIDTH), lambda i: (i, 0))],
        out_shape=[jax.ShapeDtypeStruct((FOX_WIDTH, n), jnp.bfloat16), wide, act, act,
                   jax.ShapeDtypeStruct((n, 3 * RET_WIDTH), jnp.bfloat16)],
        scratch_shapes=[
            pltpu.VMEM((3 * FOX_HEADS, LANES), jnp.float32),
            pltpu.VMEM((2, rows, FOX_WIDTH), jnp.float32),
        ],
        compiler_params=pltpu.CompilerParams(dimension_semantics=("arbitrary",), vmem_limit_bytes=VMEM_LIMIT),
        name="input_stage",
    )(x2, gain, w7, wf, fbias, cos_t, sin_t)


ONES_ROWS = 16


def _fox_kernel(q_ref, k_ref, v_ref, g_ref, o_ref, vt_ref, s_ref, bias_ref):
    blk = g_ref.shape[1] // 2
    i = pl.program_id(2)
    dh = FOX_HEAD_DIM
    heads = vt_ref.shape[0]
    sub = vt_ref.shape[3]
    subs = blk // sub
    half = blk // 2
    slots = s_ref.shape[0]

    @pl.when(i == 0)
    def _():
        key = lax.broadcasted_iota(jnp.int32, bias_ref.shape, 0)
        query = lax.broadcasted_iota(jnp.int32, bias_ref.shape, 1)
        bias_ref[...] = jnp.where(key <= query, 0.0, NEG_BIG)
        ones = jnp.ones((ONES_ROWS, sub), jnp.bfloat16)
        for jb in range(v_ref.shape[1] // sub):
            vt = jnp.transpose(v_ref[0, jb * sub:(jb + 1) * sub, :].astype(jnp.float32))
            for h in range(heads):
                vt_ref[h, jb, 0:dh, :] = vt[h * dh:(h + 1) * dh].astype(jnp.bfloat16)
                vt_ref[h, jb, dh:dh + ONES_ROWS, :] = ones

    def query_operands(qb):
        per_head = []
        for h in range(heads):
            pair, second = divmod(h, 2)
            qt = q_ref[pair * LANES:(pair + 1) * LANES, qb * blk:(qb + 1) * blk].astype(jnp.float32)
            row = lax.broadcasted_iota(jnp.int32, qt.shape, 0)
            own = jnp.where((row >= second * dh) & (row < (second + 1) * dh), qt, 0.0)
            ones = jnp.where((row >= 3 * second) & (row < 3 * second + 3), 1.0, 0.0)
            per_head.append(jnp.concatenate([own, ones], axis=0).astype(jnp.bfloat16))
        return per_head

    qts = [query_operands(0), query_operands(1)]

    def causal(s, t, lo):
        assert t * sub == lo
        return s + bias_ref[:, 0:s.shape[1]]

    def produce(unit, slot):
        j, t, h, qb, lo, mask = unit
        start = pl.multiple_of(j * blk + t * sub, sub)
        k_cols = slice(h // 2 * 2 * LANES, (h // 2 + 1) * 2 * LANES)
        s = _dot(k_ref[0, pl.ds(start, sub), k_cols], qts[qb][h][:, lo:])
        if mask == "produce":
            s = causal(s, t, lo)
        s_ref[slot, :, lo:] = s
        return jnp.max(s, axis=0, keepdims=True)

    def consume(state, unit, slot, col_max):
        j, t, h, qb, lo, mask = unit
        m, acc = state[qb][h]
        s = s_ref[slot, :, lo:]
        if mask == "consume":
            s = causal(s, t, lo)
            col_max = jnp.max(s, axis=0, keepdims=True)
        m_new = jnp.maximum(m[:, lo:], col_max)
        p = jnp.exp2(s - m_new).astype(jnp.bfloat16)
        acc_new = jnp.exp2(m[:, lo:] - m_new) * acc[:, lo:] + _dot(vt_ref[h, j * subs + t], p)
        if lo:
            m_new = jnp.concatenate([m[:, :lo], m_new], axis=1)
            acc_new = jnp.concatenate([acc[:, :lo], acc_new], axis=1)
        state[qb][h] = (m_new, acc_new)

    def finish(state, qb):
        out_t = jnp.concatenate([acc[0:dh] / acc[dh:dh + 1] for _, acc in state[qb]], axis=0)
        rows = slice(qb * blk, (qb + 1) * blk)
        o_ref[0, rows, :] = (jnp.transpose(out_t) * g_ref[0, rows, :].astype(jnp.float32)).astype(o_ref.dtype)

    def run(state, units, first_max, following=(), after=None):
        pending = dict(enumerate(first_max))
        stream = list(units) + list(following)
        for n, unit in enumerate(units):
            if n + 2 < len(stream):
                pending[n + 2] = produce(stream[n + 2], (n + 2) % slots)
            consume(state, unit, n % slots, pending.pop(n))
            if after and n in after:
                after[n]()
        return tuple(pending[n] for n in sorted(pending))

    def shared(j):
        return [(j, t, h, qb, 0, None) for t in range(subs) for qb in range(2) for h in range(heads)]

    def shared_blocks(jj, carry):
        state, first_max = carry
        state = [list(per_block) for per_block in state]
        first_max = run(state, shared(2 * jj) + shared(2 * jj + 1), first_max, following=shared(2 * jj + 2)[:2])
        return tuple(tuple(per_block) for per_block in state), first_max

    init = tuple(
        tuple((jnp.full((1, blk), NEG_BIG, jnp.float32), jnp.zeros((dh + ONES_ROWS, blk), jnp.float32))
              for _ in range(heads)) for _ in range(2))
    assert len(shared(0)) % slots == 0
    first_max = (produce(shared(0)[0], 0), produce(shared(0)[1], 1))
    state, first_max = lax.fori_loop(0, i, shared_blocks, (init, first_max))

    ja, jb = 2 * i, 2 * i + 1
    every = range(heads)
    tail = (
        [(ja, 0, h, 0, 0, "consume" if h < 2 else "produce") for h in every]
        + [(ja, 0, h, 1, 0, None) for h in every]
        + [(ja, 1, h, 0, half, "produce") for h in every]
        + [(ja, 1, h, 1, 0, None) for h in every]
        + [(jb, 0, h, 1, 0, "produce") for h in every]
        + [(jb, 1, h, 1, half, "produce") for h in every]
    )
    state = [list(per_block) for per_block in state]
    run(state, tail, first_max, after={3 * heads - 1: lambda: finish(state, 0)})
    finish(state, 1)


def _fox_stage(fq_t, fk, fv, fg):
    batch, seq, _ = fk.shape
    blk = FOX_BLOCK
    heads = FOX_STEP_HEADS
    width = heads * FOX_HEAD_DIM
    groups = FOX_HEADS // heads
    steps = seq // (2 * blk)
    tile = pl.BlockSpec((1, 2 * blk, width), lambda b, p, i: (b, i, p))
    return pl.pallas_call(
        _fox_kernel,
        grid=(batch, groups, steps),
        in_specs=[
            pl.BlockSpec((width, 2 * blk), lambda b, p, i: (p, b * steps + i)),
            pl.BlockSpec((1, seq, 2 * width), lambda b, p, i: (b, 0, p)),
            pl.BlockSpec((1, seq, width), lambda b, p, i: (b, 0, p)),
            tile,
        ],
        out_specs=tile,
        out_shape=jax.ShapeDtypeStruct((batch, seq, FOX_WIDTH), jnp.bfloat16),
        scratch_shapes=[
            pltpu.VMEM((heads, seq // FOX_KEYS, FOX_HEAD_DIM + ONES_ROWS, FOX_KEYS), jnp.bfloat16),
            pltpu.VMEM((4 * blk // FOX_KEYS, FOX_KEYS, blk), jnp.float32),
            pltpu.VMEM((FOX_KEYS, blk), jnp.float32),
        ],
        compiler_params=pltpu.CompilerParams(
            dimension_semantics=("arbitrary", "arbitrary", "arbitrary"), vmem_limit_bytes=VMEM_LIMIT),
        name="fox_stage",
    )(fq_t, fk, fv, fg)


def _ret_kernel(lg_ref, r_ref, mf_ref, wo_ref, gain_ref, x_ref, pre_gain_ref, wg_ref, o_ref,
                mixed_ref, x_prev_ref, state_ref, decay_ref, qdec_ref, kdec_ref, sdec_ref):
    blk = r_ref.shape[1]

    @pl.when((pl.program_id(0) == 0) & (pl.program_id(1) == 0))
    def _():
        row = lax.broadcasted_iota(jnp.int32, (blk, blk), 0)
        col = lax.broadcasted_iota(jnp.int32, (blk, blk), 1)
        dist = jnp.abs(row - col).astype(jnp.float32)
        pos = lax.broadcasted_iota(jnp.int32, (blk, LANES), 0).astype(jnp.float32)
        for h in range(RET_HEADS):
            lg = lg_ref[h]
            decay_ref[h] = jnp.where(col // CHUNK <= row // CHUNK, jnp.exp(lg * dist), 0.0)
            qdec_ref[h] = jnp.exp(lg * (pos + 1.0))
            kdec_ref[h] = jnp.exp(lg * (blk - 1.0 - pos))
            sdec_ref[h] = jnp.exp(lg * jnp.full((8, LANES), blk, jnp.float32))

    t = pl.program_id(1)

    @pl.when(t == 0)
    def _():
        state_ref[...] = jnp.zeros_like(state_ref)
        mixed_ref[...] = jnp.zeros_like(mixed_ref)
        x_prev_ref[...] = jnp.zeros_like(x_prev_ref)

    previous = jnp.concatenate([mf_ref[0], mixed_ref[(t + 1) % 2]], axis=1)

    def project(part):
        return _dot(previous, wo_ref[:, part * (D_MODEL // 2):(part + 1) * (D_MODEL // 2)])

    def head_cols(h, part=0):
        first = part * RET_WIDTH + h * LANES
        return slice(first, first + LANES)

    def first_matmuls(h):
        q = r_ref[0, :, head_cols(h, 0)]
        return _dot_nt(q, r_ref[0, :, head_cols(h, 1)]), _dot(q, state_ref[h].astype(jnp.bfloat16))

    def rest_of_head(h, scores, cross):
        cols = head_cols(h)
        k = r_ref[0, :, head_cols(h, 1)]
        v = r_ref[0, :, head_cols(h, 2)]
        inner = _dot((scores * decay_ref[h]).astype(jnp.bfloat16), v)
        kd = (k.astype(jnp.float32) * kdec_ref[h]).astype(jnp.bfloat16)
        state_ref[h] = state_ref[h] * sdec_ref[h, 0:1, :] + _dot_tn(kd, v)

        o = inner + cross * qdec_ref[h]
        mu = jnp.mean(o, axis=-1, keepdims=True)
        d = o - mu
        var = jnp.mean(d * d, axis=-1, keepdims=True)
        y = d * lax.rsqrt(var + GN_EPS)
        mixed_ref[t % 2, :, cols] = (y * gate_all[:, cols]).astype(mixed_ref.dtype)

    projected = [project(0)]
    pending = first_matmuls(0)
    x = x_ref[0]
    u = (x * lax.rsqrt(jnp.mean(x * x, axis=-1, keepdims=True) + NORM_EPS) * pre_gain_ref[...]).astype(jnp.bfloat16)
    z = _dot(u, wg_ref[...])
    gate_all = z * (1.0 / (1.0 + jnp.exp(-z)))
    for h in range(RET_HEADS):
        if h + 1 < RET_HEADS:
            following = first_matmuls(h + 1)
        else:
            projected.append(project(1))
        rest_of_head(h, *pending)
        pending = following

    out = jnp.concatenate(projected, axis=1)
    ms = jnp.mean(out * out, axis=-1, keepdims=True)
    o_ref[0] = x_prev_ref[...] + out * lax.rsqrt(ms + NORM_EPS) * gain_ref[...]
    x_prev_ref[...] = x_ref[0]


def _ret_output_stage(log_gamma, ret_in, mixed_fox, wo, gain, x, pre_gain, w_gate):
    batch, seq, _ = ret_in.shape
    blk = RET_BLOCK
    steps = seq // blk
    current = lambda b, t: (b, jnp.minimum(t, steps - 1), 0)
    lagging = lambda b, t: (b, jnp.maximum(t - 1, 0), 0)
    return pl.pallas_call(
        _ret_kernel,
        grid=(batch, steps + 1),
        in_specs=[
            pl.BlockSpec(memory_space=pltpu.SMEM),
            pl.BlockSpec((1, blk, 3 * RET_WIDTH), current),
            pl.BlockSpec((1, blk, FOX_WIDTH), lagging),
            pl.BlockSpec((FOX_WIDTH + RET_WIDTH, D_MODEL), lambda b, t: (0, 0)),
            pl.BlockSpec((1, D_MODEL), lambda b, t: (0, 0)),
            pl.BlockSpec((1, blk, D_MODEL), current),
            pl.BlockSpec((1, D_MODEL), lambda b, t: (0, 0)),
            pl.BlockSpec((D_MODEL, RET_WIDTH), lambda b, t: (0, 0)),
        ],
        out_specs=pl.BlockSpec((1, blk, D_MODEL), lagging),
        out_shape=jax.ShapeDtypeStruct((batch, seq, D_MODEL), jnp.float32),
        scratch_shapes=[
            pltpu.VMEM((2, blk, RET_WIDTH), jnp.bfloat16),
            pltpu.VMEM((blk, D_MODEL), jnp.float32),
            pltpu.VMEM((RET_HEADS, RET_HEAD_DIM, RET_HEAD_DIM), jnp.float32),
            pltpu.VMEM((RET_HEADS, blk, blk), jnp.float32),
            pltpu.VMEM((RET_HEADS, blk, LANES), jnp.float32),
            pltpu.VMEM((RET_HEADS, blk, LANES), jnp.float32),
            pltpu.VMEM((RET_HEADS, 8, LANES), jnp.float32),
        ],
        compiler_params=pltpu.CompilerParams(
            dimension_semantics=("arbitrary", "arbitrary"), vmem_limit_bytes=VMEM_LIMIT),
        name="ret_output_stage",
    )(log_gamma, ret_in, mixed_fox, wo, gain, x, pre_gain, w_gate)


def _layer(h, pre_gain, w_in, forget_bias, w_out, post_gain, cos_t, sin_t, log_gamma):
    batch, seq, _ = h.shape
    x2 = h.reshape(batch * seq, D_MODEL)
    wb = w_in.astype(jnp.bfloat16)
    fox_cols = wb[:, :4 * FOX_WIDTH].reshape(D_MODEL, 4, FOX_WIDTH)
    ret_cols = wb[:, 4 * FOX_WIDTH + FOX_HEADS:].reshape(D_MODEL, 4, RET_WIDTH)
    w8 = jnp.concatenate([fox_cols, ret_cols], axis=1).transpose(1, 0, 2)
    wf = jnp.pad(jnp.repeat(wb[:, 4 * FOX_WIDTH:4 * FOX_WIDTH + FOX_HEADS], 3, axis=1),
                 ((0, 0), (0, LANES - 3 * FOX_HEADS)))
    fbias = jnp.pad(jnp.repeat(forget_bias.astype(jnp.float32), 3), (0, LANES - 3 * FOX_HEADS))[None, :]

    fq, fk, fv, fg, ret_in = _input_stage(
        x2, pre_gain[None, :], w8[:7], wf, fbias, cos_t, sin_t, seq)

    def seq3(a):
        return a.reshape(batch, seq, a.shape[-1])

    mixed_fox = _fox_stage(fq, seq3(fk), seq3(fv), seq3(fg))
    return _ret_output_stage(log_gamma, seq3(ret_in), mixed_fox, w_out.astype(jnp.bfloat16),
                             post_gain[None, :], h, pre_gain[None, :], w8[7])


def kernel(x, pre_norm_gain, w_in, fox_forget_bias, w_out, post_norm_gain):
    seq = x.shape[1]
    cos_t, sin_t = _rope_tables(seq)
    log_gamma = jnp.log1p(
        -jnp.exp(jnp.linspace(math.log(1.0 / 32), math.log(1.0 / 512), RET_HEADS))).astype(jnp.float32)
    h = x
    for layer in range(pre_norm_gain.shape[0]):
        h = _layer(h, pre_norm_gain[layer], w_in[layer], fox_forget_bias[layer], w_out[layer],
                   post_norm_gain[layer], cos_t, sin_t, log_gamma)
    return h
```
